```python
import math, functools
import jax, jax.numpy as jnp
from jax import lax
import numpy as np

D_MODEL = 1024
BATCH = 4
SEQ = 4096
DEPTH = 2

GRID_W = 64
CTX_LEN = 256
N_ADA = 9
FF_DIM = 2816
EPS = 1e-6
A_HEADS = 4
A_DK = 128
A_DV = 128
A_DIM = A_HEADS * A_DK
A_CHUNK = 32
B_HEADS = 8
B_HEADDIM = 64
B_DIM = B_HEADS * B_HEADDIM
B_GROUPS = 2
B_STATE = 128
B_CONV = 3
B_CHUNK = 128
B_XBC = B_DIM + 2 * B_GROUPS * B_STATE
EVEN_SPLITS = (A_DIM, A_DIM, 2 * A_DIM, A_DIM, B_DIM, B_XBC, 2 * B_HEADS)
EVEN_IN = A_DIM * 5 + B_DIM + B_XBC + 2 * B_HEADS
EVEN_MIX = A_DIM + B_DIM
C_HEAD = 64
C_HEADS = D_MODEL // C_HEAD
DECAY_LORA = 64
AAA_LORA = 64
GATE_LORA = 128
LNX_EPS = 64e-5
N_EVEN = (DEPTH + 1) // 2
N_ODD = DEPTH // 2

kernel_name = 'hybrid_hgrn2_ssd_rwkv7_macaron_prefix'


def rmsnorm(x, g, eps=EPS):
    xf = x.astype(jnp.float32)
    y = xf * lax.rsqrt(jnp.mean(xf * xf, axis=-1, keepdims=True) + eps)
    return y.astype(x.dtype) * g


def swiglu(h, wi, wo):
    gt, up = jnp.split(h @ wi, 2, axis=-1)
    return (jax.nn.silu(gt) * up) @ wo


def macaron_half(xs, m, slot, g, wi, wo):
    h = rmsnorm(xs, g) * (1.0 + m[3 * slot + 1]) + m[3 * slot]
    return xs + 0.5 * m[3 * slot + 2] * swiglu(h, wi, wo)


def flip_t(z):
    return jnp.flip(z, axis=1)


def dwconv_centred(u, w, b):
    kw = w.shape[0]
    half = kw // 2
    t = u.shape[1]
    up = jnp.pad(u, ((0, 0), (half, half), (0, 0)))
    return sum(w[j] * up[:, j:j + t] for j in range(kw)) + b


def shift_grid(h):
    bsz, t, ch = h.shape
    rows = t // GRID_W
    q = ch // 4
    g = h.reshape(bsz, rows, GRID_W, ch)
    left = jnp.pad(g[:, :, :-1, :q], ((0, 0), (0, 0), (1, 0), (0, 0)))
    right = jnp.pad(g[:, :, 1:, q:2 * q], ((0, 0), (0, 0), (0, 1), (0, 0)))
    up = jnp.pad(g[:, :-1, :, 2 * q:3 * q], ((0, 0), (1, 0), (0, 0), (0, 0)))
    down = jnp.pad(g[:, 1:, :, 3 * q:], ((0, 0), (0, 1), (0, 0), (0, 0)))
    return jnp.concatenate([left, right, up, down], axis=-1).reshape(bsz, t, ch)


def shift_seq(h):
    ch = h.shape[-1]
    prev = jnp.pad(h[:, :-1], ((0, 0), (1, 0), (0, 0)))
    nxt = jnp.pad(h[:, 1:], ((0, 0), (0, 1), (0, 0)))
    from_prev = (jnp.arange(ch) // (ch // 4)) % 2 == 0
    return jnp.where(from_prev, prev, nxt)


def hgrn2_chunk_scan(q, k, v, logf, s0, with_output):
    f32 = jnp.float32
    bsz, t, h, dk = k.shape
    dv = v.shape[-1]
    L = A_CHUNK
    nc = t // L
    k = k.astype(f32).reshape(bsz, nc, L, h, dk)
    v = v.astype(f32).reshape(bsz, nc, L, h, dv)
    bcum = jnp.cumsum(logf.astype(f32).reshape(bsz, nc, L, h, dk), axis=2)
    blast = bcum[:, :, -1]
    ds = jnp.einsum('bclhk,bclhv->bchkv', k * jnp.exp(blast[:, :, None] - bcum), v)

    def step(s, inp):
        dec, d = inp
        return dec[..., None] * s + d, s

    s_fin, s_start = lax.scan(step, s0, (jnp.moveaxis(jnp.exp(blast), 1, 0), jnp.moveaxis(ds, 1, 0)))
    if not with_output:
        return None, s_fin
    s_start = jnp.moveaxis(s_start, 0, 1)
    q = q.astype(f32).reshape(bsz, nc, L, h, dk)
    bref = bcum[:, :, L // 2 - 1:L // 2]
    scores = jnp.einsum('bclhk,bcshk->bchls', q * jnp.exp(bcum - bref), k * jnp.exp(bref - bcum))
    upto = jnp.tril(jnp.ones((L, L), bool))
    scores = jnp.where(upto, scores, 0.0)
    o = (jnp.einsum('bchls,bcshv->bclhv', scores, v)
         + jnp.einsum('bclhk,bchkv->bclhv', q * jnp.exp(bcum), s_start))
    return o.reshape(bsz, t, h, dv), s_fin


def ssd_chunk_scan(x, dt, bm, cm, s0, with_output, a):
    f32 = jnp.float32
    bsz, t, g, hg, p = x.shape
    n = bm.shape[-1]
    L = B_CHUNK
    nc = t // L
    x = x.astype(f32).reshape(bsz, nc, L, g, hg, p)
    dt = dt.astype(f32).reshape(bsz, nc, L, g, hg)
    bm = bm.astype(f32).reshape(bsz, nc, L, g, n)
    cm = cm.astype(f32).reshape(bsz, nc, L, g, n)
    acum = jnp.cumsum(dt * a, axis=2)
    alast = acum[:, :, -1]
    states = jnp.einsum('bclgn,bclgh,bclghp->bcghpn', bm, jnp.exp(alast[:, :, None] - acum) * dt, x)

    def step(s, inp):
        dec, d = inp
        return dec[..., None, None] * s + d, s

    s_fin, s_start = lax.scan(step, s0, (jnp.moveaxis(jnp.exp(alast), 1, 0), jnp.moveaxis(states, 1, 0)))
    if not with_output:
        return None, s_fin
    s_start = jnp.moveaxis(s_start, 0, 1)
    seg = acum[:, :, :, None] - acum[:, :, None, :]
    upto = jnp.tril(jnp.ones((L, L), bool))[:, :, None, None]
    lmat = jnp.exp(jnp.where(upto, seg, -jnp.inf))
    cb = jnp.einsum('bclgn,bcsgn->bclsg', cm, bm)
    y = (jnp.einsum('bclsg,bclsgh,bcsgh,bcsghp->bclghp', cb, lmat, dt, x)
         + jnp.einsum('bclgn,bcghpn,bclgh->bclghp', cm, s_start, jnp.exp(acum)))
    return y.reshape(bsz, t, g, hg, p), s_fin


def rwkv7_scan(r, w, k, v, a, b, s0, with_output):
    f32 = jnp.float32

    def tm(z):
        return jnp.moveaxis(z.astype(f32), 1, 0)

    def update(s, w_t, k_t, v_t, a_t, b_t):
        sa = jnp.einsum('bhij,bhj->bhi', s, a_t)
        return s * w_t[:, :, None, :] + sa[..., None] * b_t[:, :, None, :] + v_t[..., None] * k_t[:, :, None, :]

    if with_output:
        def step(s, inp):
            r_t, w_t, k_t, v_t, a_t, b_t = inp
            s = update(s, w_t, k_t, v_t, a_t, b_t)
            return s, jnp.einsum('bhij,bhj->bhi', s, r_t)
        s_fin, y = lax.scan(step, s0, (tm(r), tm(w), tm(k), tm(v), tm(a), tm(b)))
        return jnp.moveaxis(y, 0, 1), s_fin

    def step_state(s, inp):
        return update(s, *inp), None
    s_fin, _ = lax.scan(step_state, s0, (tm(w), tm(k), tm(v), tm(a), tm(b)))
    return None, s_fin


def prefix_scan(scan_fn, ctx_in, lat_in, s0, need_ctx_out, reverse):
    if reverse:
        ctx_in = [flip_t(z) for z in ctx_in]
        lat_in = [flip_t(z) for z in lat_in]
    y_ctx, s_ctx = scan_fn(*ctx_in, s0, need_ctx_out)
    y_lat, _ = scan_fn(*lat_in, s_ctx, True)
    if reverse:
        y_lat = flip_t(y_lat)
        y_ctx = None if y_ctx is None else flip_t(y_ctx)
    return y_ctx, y_lat


def even_mixer(h_lat, h_ctx, in_w, out_w, lb, hgrn_g, conv_w, conv_b, dt_bias, a_log, d_skip, ssd_g, need_ctx_out):
    f32 = jnp.float32
    cuts = [int(v) for v in np.cumsum(EVEN_SPLITS)[:-1]]
    hg = B_HEADS // B_GROUPS

    def features(h):
        bsz, t, _ = h.shape
        q, i, f, g, z, xbc, dt = jnp.split(h @ in_w, cuts, axis=-1)
        f = lb + (1.0 - lb) * jax.nn.sigmoid(f.astype(f32).reshape(bsz, t, 2, A_HEADS, A_DK))
        xbc = jax.nn.silu(dwconv_centred(xbc, conv_w, conv_b))
        xs, bm, cm = jnp.split(xbc, [B_DIM, B_DIM + B_GROUPS * B_STATE], axis=-1)
        return dict(
            q=jax.nn.silu(q).reshape(bsz, t, A_HEADS, A_DK),
            i=i.reshape(bsz, t, A_HEADS, A_DV),
            k=1.0 - f, logf=jnp.log(f), g=g, z=z,
            x=xs.reshape(bsz, t, B_GROUPS, hg, B_HEADDIM),
            b=bm.reshape(bsz, t, B_GROUPS, B_STATE),
            c=cm.reshape(bsz, t, B_GROUPS, B_STATE),
            dt=jax.nn.softplus(dt.astype(f32).reshape(bsz, t, 2, B_GROUPS, hg)
                               + dt_bias.astype(f32).reshape(2, B_GROUPS, hg)))

    fc, fl = features(h_ctx), features(h_lat)
    bsz = h_lat.shape[0]
    a = -jnp.exp(a_log.astype(f32)).reshape(2, B_GROUPS, hg)
    s0_a = jnp.zeros((bsz, A_HEADS, A_DK, A_DV), f32)
    s0_b = jnp.zeros((bsz, B_GROUPS, hg, B_HEADDIM, B_STATE), f32)
    ya, yb = [], []
    for d in range(2):
        ya.append(prefix_scan(hgrn2_chunk_scan,
                              [fc['q'], fc['k'][:, :, d], fc['i'], fc['logf'][:, :, d]],
                              [fl['q'], fl['k'][:, :, d], fl['i'], fl['logf'][:, :, d]],
                              s0_a, need_ctx_out, d == 1))
        yb.append(prefix_scan(functools.partial(ssd_chunk_scan, a=a[d]),
                              [fc['x'], fc['dt'][:, :, d], fc['b'], fc['c']],
                              [fl['x'], fl['dt'][:, :, d], fl['b'], fl['c']],
                              s0_b, need_ctx_out, d == 1))

    def finish(fe, oa, ob, dtype):
        bsz_, t = oa.shape[:2]
        oa = rmsnorm(oa, hgrn_g).reshape(bsz_, t, A_DIM) * jax.nn.silu(fe['g'].astype(f32))
        ob = ob + d_skip.astype(f32).reshape(B_GROUPS, hg, 1) * fe['x']
        ob = ob.reshape(bsz_, t, B_GROUPS, hg * B_HEADDIM) * jax.nn.silu(fe['z'].astype(f32)).reshape(bsz_, t, B_GROUPS, hg * B_HEADDIM)
        ob = rmsnorm(ob, ssd_g.reshape(B_GROUPS, hg * B_HEADDIM)).reshape(bsz_, t, B_DIM)
        return jnp.concatenate([oa, ob], axis=-1).astype(dtype) @ out_w

    y_lat = finish(fl, ya[0][1] + ya[1][1], yb[0][1] + yb[1][1], h_lat.dtype)
    y_ctx = finish(fc, ya[0][0] + ya[1][0], yb[0][0] + yb[1][0], h_ctx.dtype) if need_ctx_out else None
    return y_lat, y_ctx


def odd_mixer(h_lat, h_ctx, x_mix, wr, wk, wv, wo, w0, w1, w2, a0, a1, a2, g1, g2, k_k, k_a, r_k,
              ln_w, ln_b, need_ctx_out):
    f32 = jnp.float32

    def heads(z):
        return z.reshape(*z.shape[:-1], C_HEADS, C_HEAD)

    def features(h, h_shift):
        xx = h_shift - h
        xr, xw, xk, xv, xa, xg = [h + xx * x_mix[j] for j in range(6)]
        k = (xk @ wk).astype(f32)
        zw = w0[:, None, None].astype(f32) + jnp.einsum(
            'dbtr,drc->dbtc', jnp.tanh(jnp.einsum('btc,dcr->dbtr', xw, w1)), w2).astype(f32)
        za = a0[:, None, None] + jnp.einsum('dbtr,drc->dbtc', jnp.einsum('btc,dcr->dbtr', xa, a1), a2)
        a = jax.nn.sigmoid(za.astype(f32))
        kk = heads(k * k_k)
        kk = kk * lax.rsqrt(jnp.maximum(jnp.sum(kk * kk, axis=-1, keepdims=True), 1e-24))
        return dict(r=heads(xr @ wr), v=heads(xv @ wv),
                    g=jax.nn.sigmoid(xg @ g1) @ g2,
                    decay=heads(jnp.exp(-jnp.exp(-jax.nn.softplus(-zw) - 0.5))),
                    k=heads(k * (1.0 + (a - 1.0) * k_a)), a=heads(a), kk=kk)

    fc = features(h_ctx, shift_seq(h_ctx))
    fl = features(h_lat, shift_grid(h_lat))
    bsz = h_lat.shape[0]
    s0 = jnp.zeros((bsz, C_HEADS, C_HEAD, C_HEAD), f32)

    def scan_inputs(fe, d):
        return [fe['r'], fe['decay'][d], fe['k'][d], fe['v'], -fe['kk'], fe['kk'] * fe['a'][d]]

    outs = [prefix_scan(rwkv7_scan, scan_inputs(fc, d), scan_inputs(fl, d), s0, need_ctx_out, d == 1)
            for d in range(2)]

    def finish(fe, y, dtype):
        bsz_, t = y.shape[:2]
        mu = jnp.mean(y, axis=-1, keepdims=True)
        var = jnp.mean(jnp.square(y - mu), axis=-1, keepdims=True)
        yn = ((y - mu) * lax.rsqrt(var + LNX_EPS)).reshape(bsz_, t, D_MODEL) * ln_w + ln_b
        bonus = jnp.sum(fe['r'].astype(f32)[None] * fe['k'] * r_k, axis=-1, keepdims=True).sum(0) * fe['v']
        yn = yn + bonus.reshape(bsz_, t, D_MODEL)
        return (yn * fe['g']).astype(dtype) @ wo

    y_lat = finish(fl, outs[0][1] + outs[1][1], h_lat.dtype)
    y_ctx = finish(fc, outs[0][0] + outs[1][0], h_ctx.dtype) if need_ctx_out else None
    return y_lat, y_ctx


def setup_inputs(seed: int = 0) -> dict:
    key = jax.random.key(seed)
    ks = iter(jax.random.split(key, 64))
    f32 = jnp.float32

    def nrm(shape, std):
        return std * jax.random.normal(next(ks), shape, f32)

    def gain(shape):
        return 1.0 + nrm(shape, 0.02)

    def unif(shape, lo, hi):
        return jax.random.uniform(next(ks), shape, f32, lo, hi)

    D = D_MODEL
    dt0 = jnp.exp(unif((N_EVEN, 2, B_HEADS), math.log(1e-3), math.log(1e-1)))
    return {
        'x': nrm((BATCH, SEQ, D), 1.0),
        'c': nrm((BATCH, D), 1.0),
        'ctx': nrm((BATCH, CTX_LEN, D), 1.0),
        'c_ctx': nrm((D,), 1.0),
        'ada_w': nrm((DEPTH, D, N_ADA * D), 0.5 * D ** -0.5),
        'ada_b': nrm((DEPTH, N_ADA * D), 0.01),
        'norm_g': gain((DEPTH, 3, D)),
        'ffn_wi': nrm((DEPTH, 2, D, 2 * FF_DIM), D ** -0.5),
        'ffn_wo': nrm((DEPTH, 2, FF_DIM, D), FF_DIM ** -0.5),
        'final_g': gain((D,)),
        'hgrn_lb_logits': nrm((DEPTH + 1, 2, A_DIM), 0.1),
        'ev_in_w': nrm((N_EVEN, D, EVEN_IN), D ** -0.5),
        'ev_out_w': nrm((N_EVEN, EVEN_MIX, D), EVEN_MIX ** -0.5),
        'hgrn_norm_g': gain((N_EVEN, A_DV)),
        'ssd_conv_w': nrm((N_EVEN, B_CONV, B_XBC), B_CONV ** -0.5),
        'ssd_conv_b': nrm((N_EVEN, B_XBC), 0.01),
        'ssd_dt_bias': dt0 + jnp.log(-jnp.expm1(-dt0)),
        'ssd_a_log': jnp.log(unif((N_EVEN, 2, B_HEADS), 1.0, 16.0)),
        'ssd_d': 1.0 + nrm((N_EVEN, B_HEADS), 0.1),
        'ssd_norm_g': gain((N_EVEN, B_DIM)),
        'rw_x_mix': unif((N_ODD, 6, D), 0.0, 1.0),
        'rw_wr': nrm((N_ODD, D, D), D ** -0.5),
        'rw_wk': nrm((N_ODD, D, D), D ** -0.5),
        'rw_wv': nrm((N_ODD, D, D), D ** -0.5),
        'rw_wo': nrm((N_ODD, D, D), D ** -0.5),
        'rw_w0': unif((N_ODD, 2, D), -6.0, 1.0),
        'rw_w1': nrm((N_ODD, 2, D, DECAY_LORA), D ** -0.5),
        'rw_w2': nrm((N_ODD, 2, DECAY_LORA, D), 0.5 * DECAY_LORA ** -0.5),
        'rw_a0': nrm((N_ODD, 2, D), 0.1),
        'rw_a1': nrm((N_ODD, 2, D, AAA_LORA), D ** -0.5),
        'rw_a2': nrm((N_ODD, 2, AAA_LORA, D), 0.5 * AAA_LORA ** -0.5),
        'rw_g1': nrm((N_ODD, D, GATE_LORA), D ** -0.5),
        'rw_g2': nrm((N_ODD, GATE_LORA, D), GATE_LORA ** -0.5),
        'rw_k_k': 0.85 + nrm((N_ODD, D), 0.05),
        'rw_k_a': 1.0 + nrm((N_ODD, D), 0.05),
        'rw_r_k': nrm((N_ODD, C_HEADS, C_HEAD), 0.1),
        'rw_ln_w': gain((N_ODD, D)),
        'rw_ln_b': nrm((N_ODD, D), 0.01),
    }


def reference(x, c, ctx, c_ctx, ada_w, ada_b, norm_g, ffn_wi, ffn_wo, final_g, hgrn_lb_logits,
              ev_in_w, ev_out_w, hgrn_norm_g, ssd_conv_w, ssd_conv_b, ssd_dt_bias, ssd_a_log, ssd_d,
              ssd_norm_g, rw_x_mix, rw_wr, rw_wk, rw_wv, rw_wo, rw_w0, rw_w1, rw_w2, rw_a0, rw_a1,
              rw_a2, rw_g1, rw_g2, rw_k_k, rw_k_a, rw_r_k, rw_ln_w, rw_ln_b):
    f32 = jnp.float32
    bsz = x.shape[0]
    lb_all = jnp.cumsum(jax.nn.softmax(hgrn_lb_logits.astype(f32), axis=0), axis=0)
    sc = jax.nn.silu(c)
    scc = jax.nn.silu(c_ctx)
    x_lat, x_ctx = x, ctx
    for layer in range(DEPTH):
        last = layer == DEPTH - 1
        m_lat = (sc @ ada_w[layer] + ada_b[layer]).reshape(bsz, N_ADA, D_MODEL).transpose(1, 0, 2)[:, :, None, :]
        m_ctx = (scc @ ada_w[layer] + ada_b[layer]).reshape(N_ADA, D_MODEL)
        g = norm_g[layer]
        x_lat = macaron_half(x_lat, m_lat, 0, g[0], ffn_wi[layer, 0], ffn_wo[layer, 0])
        x_ctx = macaron_half(x_ctx, m_ctx, 0, g[0], ffn_wi[layer, 0], ffn_wo[layer, 0])
        h_lat = rmsnorm(x_lat, g[1]) * (1.0 + m_lat[4]) + m_lat[3]
        h_ctx = rmsnorm(x_ctx, g[1]) * (1.0 + m_ctx[4]) + m_ctx[3]
        if layer % 2 == 0:
            e = layer // 2
            y_lat, y_ctx = even_mixer(h_lat, h_ctx, ev_in_w[e], ev_out_w[e],
                                      lb_all[layer].reshape(2, A_HEADS, A_DK), hgrn_norm_g[e],
                                      ssd_conv_w[e], ssd_conv_b[e], ssd_dt_bias[e], ssd_a_log[e],
                                      ssd_d[e], ssd_norm_g[e], not last)
        else:
            o = layer // 2
            y_lat, y_ctx = odd_mixer(h_lat, h_ctx, rw_x_mix[o], rw_wr[o], rw_wk[o], rw_wv[o], rw_wo[o],
                                     rw_w0[o], rw_w1[o], rw_w2[o], rw_a0[o], rw_a1[o], rw_a2[o],
                                     rw_g1[o], rw_g2[o], rw_k_k[o], rw_k_a[o], rw_r_k[o],
                                     rw_ln_w[o], rw_ln_b[o], not last)
        x_lat = x_lat + m_lat[5] * y_lat
        x_lat = macaron_half(x_lat, m_lat, 2, g[2], ffn_wi[layer, 1], ffn_wo[layer, 1])
        if not last:
            x_ctx = x_ctx + m_ctx[5] * y_ctx
            x_ctx = macaron_half(x_ctx, m_ctx, 2, g[2], ffn_wi[layer, 1], ffn_wo[layer, 1])
    return rmsnorm(x_lat, final_g)
```

```python
import functools

import numpy as np
import jax
import jax.numpy as jnp
from jax import lax
from jax.experimental import pallas as pl
from jax.experimental.pallas import tpu as pltpu

F32 = jnp.float32
BF16 = jnp.bfloat16

D_MODEL = 1024
N_ADA = 9
FF_DIM = 2816
EPS = 1e-6
GRID_W = 64
A_HEADS = 4
A_DK = 128
A_DIM = A_HEADS * A_DK
A_CHUNK = 32
B_HEADS = 8
B_HEADDIM = 64
B_DIM = B_HEADS * B_HEADDIM
B_GROUPS = 2
B_STATE = 128
B_CHUNK = 128
B_XBC = B_DIM + 2 * B_GROUPS * B_STATE
EVEN_IN = A_DIM * 5 + B_DIM + B_XBC + 2 * B_HEADS
EVEN_IN_PAD = 4224
C_HEAD = 64
C_HEADS = D_MODEL // C_HEAD
C_PAIRS = C_HEADS // 2
C_CHUNK = 64
LNX_EPS = 64e-5
LANES = 128

VMEM_LIMIT = 56 * 1024 * 1024


def _cparams(sem):
    return pltpu.CompilerParams(dimension_semantics=sem, vmem_limit_bytes=VMEM_LIMIT)


def _sigmoid(x):
    return 1.0 / (1.0 + jnp.exp(-x))


def _silu(x):
    return x * _sigmoid(x)


def _softplus(x):
    return jnp.maximum(x, 0.0) + jnp.log1p(jnp.exp(-jnp.abs(x)))


def _dot(a, b, dims):
    return lax.dot_general(a, b, (dims, ((), ())), preferred_element_type=F32)


def _mm(a, b):
    return _dot(a.astype(BF16), b.astype(BF16), ((1,), (0,)))


def _mm_nt(a, b):
    return _dot(a.astype(BF16), b.astype(BF16), ((1,), (1,)))


def _mm_tn(a, b):
    return _dot(a.astype(BF16), b.astype(BF16), ((0,), (0,)))


def _split2(x):
    hi = x.astype(BF16)
    lo = (x - hi.astype(F32)).astype(BF16)
    return hi, lo


def _split3(x):
    x1 = x.astype(BF16)
    r1 = x - x1.astype(F32)
    x2 = r1.astype(BF16)
    x3 = (r1 - x2.astype(F32)).astype(BF16)
    return x1, x2, x3


def _mm_exact_l(m_bf, x):
    x1, x2, x3 = _split3(x)
    dn = ((1,), (0,))
    return _dot(m_bf, x1, dn) + _dot(m_bf, x2, dn) + _dot(m_bf, x3, dn)


def _mm_exact_r(x, m_bf):
    x1, x2, x3 = _split3(x)
    dn = ((1,), (0,))
    return _dot(x1, m_bf, dn) + _dot(x2, m_bf, dn) + _dot(x3, m_bf, dn)


def _mm_hp(a, b):
    ah, al = _split2(a)
    bh, bl = _split2(b)
    dn = ((1,), (0,))
    return _dot(ah, bh, dn) + _dot(ah, bl, dn) + _dot(al, bh, dn)


def _rms(x):
    return x * lax.rsqrt(jnp.mean(x * x, axis=-1, keepdims=True) + EPS)


def _iota(shape, dim):
    return lax.broadcasted_iota(jnp.int32, shape, dim)


def _mod_row(i, tm, t_lat, bsz):
    return jnp.minimum((i * tm) // t_lat, bsz)


def _seq_pos(r0, t_lat, t_ctx, n_lat):
    in_lat = r0 < n_lat
    pos = jnp.where(in_lat, r0 % t_lat, (r0 - n_lat) % t_ctx)
    slen = jnp.where(in_lat, t_lat, t_ctx)
    return in_lat, pos, slen


def _scan_block(b, s, nlat, nctx, bsz, reverse):
    if reverse:
        cblk = bsz * nlat + b * nctx + (nctx - 1 - s)
        lblk = b * nlat + (nlat - 1 - (s - nctx))
    else:
        cblk = bsz * nlat + b * nctx + s
        lblk = b * nlat + (s - nctx)
    return jnp.where(s < nctx, cblk, lblk)


def _mod_kernel(c_ref, w_ref, b_ref, o_ref):
    sc = _silu(c_ref[...])
    o_ref[...] = _mm(sc, w_ref[...]) + b_ref[...]


def _modulation(cvec, ada_w, ada_b):
    depth, d, n = ada_w.shape
    tn = 1536
    return pl.pallas_call(
        _mod_kernel,
        grid=(depth, n // tn),
        in_specs=[pl.BlockSpec((8, d), lambda l, j: (0, 0)),
                  pl.BlockSpec((None, d, tn), lambda l, j: (l, 0, j)),
                  pl.BlockSpec((None, 1, tn), lambda l, j: (l, 0, j))],
        out_specs=pl.BlockSpec((None, 8, tn), lambda l, j: (l, 0, j)),
        out_shape=jax.ShapeDtypeStruct((depth, 8, n), F32),
        compiler_params=_cparams(("arbitrary", "arbitrary")),
        name="ada_mod",
    )(cvec, ada_w, ada_b.reshape(depth, 1, n))


def _ffn_kernel(*refs, slot, nf, post):
    x_ref, m_ref, g_ref, wg_ref, wu_ref, wo_ref = refs[:6]
    if post == "prenorm":
        g2_ref, o_ref, h_ref, hs, acc = refs[6:]
    elif post == "final":
        g2_ref, o_ref, hs, acc = refs[6:]
    else:
        o_ref, hs, acc = refs[6:]
    f = pl.program_id(1)

    @pl.when(f == 0)
    def _():
        x = x_ref[...]
        h = _rms(x) * g_ref[...] * (1.0 + m_ref[3 * slot + 1:3 * slot + 2, :]) + m_ref[3 * slot:3 * slot + 1, :]
        hs[...] = h.astype(BF16)
        acc[...] = jnp.zeros_like(acc)

    h = hs[...]
    gt = _dot(h, wg_ref[...], ((1,), (0,)))
    up = _dot(h, wu_ref[...], ((1,), (0,)))
    act = (_silu(gt) * up).astype(BF16)
    acc[...] += _dot(act, wo_ref[...], ((1,), (0,)))

    @pl.when(f == nf - 1)
    def _():
        xn = x_ref[...] + 0.5 * m_ref[3 * slot + 2:3 * slot + 3, :] * acc[...]
        if post == "final":
            o_ref[...] = _rms(xn) * g2_ref[...]
        else:
            o_ref[...] = xn
        if post == "prenorm":
            h2 = _rms(xn) * g2_ref[...] * (1.0 + m_ref[4:5, :]) + m_ref[3:4, :]
            h_ref[...] = h2.astype(h_ref.dtype)


def _ffn_half(x, mod, g, wi, wo, *, slot, n_rows, t_lat, bsz, post=None, g2=None, h_dtype=F32):
    d = x.shape[1]
    tm = 1024
    while n_rows % tm or t_lat % tm:
        tm //= 2
    tf = 256
    nf = FF_DIM // tf
    in_specs = [pl.BlockSpec((tm, d), lambda i, f: (i, 0)),
                pl.BlockSpec((None, N_ADA, d), lambda i, f: (_mod_row(i, tm, t_lat, bsz), 0, 0)),
                pl.BlockSpec((1, d), lambda i, f: (0, 0)),
                pl.BlockSpec((d, tf), lambda i, f: (0, f)),
                pl.BlockSpec((d, tf), lambda i, f: (0, nf + f)),
                pl.BlockSpec((tf, d), lambda i, f: (f, 0))]
    args = [x, mod, g.reshape(1, d), wi, wi, wo]
    out_spec = pl.BlockSpec((tm, d), lambda i, f: (i, 0))
    out_shape = jax.ShapeDtypeStruct((n_rows, d), F32)
    if post is not None:
        in_specs.append(pl.BlockSpec((1, d), lambda i, f: (0, 0)))
        args.append(g2.reshape(1, d))
    if post == "prenorm":
        out_specs = [out_spec, out_spec]
        out_shapes = [out_shape, jax.ShapeDtypeStruct((n_rows, d), h_dtype)]
    else:
        out_specs, out_shapes = out_spec, out_shape
    return pl.pallas_call(
        functools.partial(_ffn_kernel, slot=slot, nf=nf, post=post),
        grid=(n_rows // tm, nf),
        in_specs=in_specs,
        out_specs=out_specs,
        out_shape=out_shapes,
        scratch_shapes=[pltpu.VMEM((tm, d), BF16), pltpu.VMEM((tm, d), F32)],
        compiler_params=_cparams(("parallel", "arbitrary")),
        name="ffn_half",
    )(*args)


def _even_in_kernel(h_ref, w_ref, lb_ref, dtb_ref, qi_ref, f_ref, gz_ref, xbc_ref, dt_ref):
    h = h_ref[...]
    dn = ((1,), (0,))
    qi = _dot(h, w_ref[:, 0:1024], dn)
    qi_ref[:, 0:512] = _silu(qi[:, 0:512])
    qi_ref[:, 512:1024] = qi[:, 512:1024]
    lb = lb_ref[...]
    f_ref[...] = lb + (1.0 - lb) * _sigmoid(_dot(h, w_ref[:, 1024:2048], dn))
    gz_ref[...] = _silu(_dot(h, w_ref[:, 2048:3072], dn))
    xbc_ref[...] = _dot(h, w_ref[:, 3072:4096], dn)
    dt_ref[...] = _softplus(_dot(h, w_ref[:, 4096:4224], dn) + dtb_ref[...])


def _even_in(h, w, lb, dtb, n_rows):
    d = h.shape[1]
    tm = 512
    while n_rows % tm:
        tm //= 2
    widths = (1024, 1024, 1024, 1024, LANES)
    return pl.pallas_call(
        _even_in_kernel,
        grid=(n_rows // tm,),
        in_specs=[pl.BlockSpec((tm, d), lambda i: (i, 0)),
                  pl.BlockSpec((d, EVEN_IN_PAD), lambda i: (0, 0)),
                  pl.BlockSpec((1, 1024), lambda i: (0, 0)),
                  pl.BlockSpec((1, LANES), lambda i: (0, 0))],
        out_specs=[pl.BlockSpec((tm, wd), lambda i: (i, 0)) for wd in widths],
        out_shape=[jax.ShapeDtypeStruct((n_rows, wd), F32) for wd in widths],
        compiler_params=_cparams(("parallel",)),
        name="even_in",
    )(h, w, lb, dtb)


def _conv_kernel(x_ref, p_ref, n_ref, w_ref, b_ref, o_ref, *, tm, t_lat, t_ctx, n_lat):
    r0 = pl.program_id(0) * tm
    _, pos, slen = _seq_pos(r0, t_lat, t_ctx, n_lat)
    first = pos == 0
    last = pos + tm == slen
    x = x_ref[...]
    row = _iota((tm, 1), 0)
    prev_row = jnp.where(first, 0.0, p_ref[7:8, :])
    next_row = jnp.where(last, 0.0, n_ref[0:1, :])
    xp = jnp.where(row == 0, prev_row, pltpu.roll(x, 1, 0))
    xn = jnp.where(row == tm - 1, next_row, pltpu.roll(x, tm - 1, 0))
    y = w_ref[0:1, :] * xp + w_ref[1:2, :] * x + w_ref[2:3, :] * xn + b_ref[...]
    o_ref[...] = _silu(y)


def _conv_silu(xbc, w, b, *, n_rows, t_lat, t_ctx, n_lat):
    c = xbc.shape[1]
    tm = 256
    nb8 = n_rows // 8
    return pl.pallas_call(
        functools.partial(_conv_kernel, tm=tm, t_lat=t_lat, t_ctx=t_ctx, n_lat=n_lat),
        grid=(n_rows // tm,),
        in_specs=[pl.BlockSpec((tm, c), lambda i: (i, 0)),
                  pl.BlockSpec((8, c), lambda i: (jnp.maximum(i * (tm // 8) - 1, 0), 0)),
                  pl.BlockSpec((8, c), lambda i: (jnp.minimum((i + 1) * (tm // 8), nb8 - 1), 0)),
                  pl.BlockSpec((3, c), lambda i: (0, 0)),
                  pl.BlockSpec((1, c), lambda i: (0, 0))],
        out_specs=pl.BlockSpec((tm, c), lambda i: (i, 0)),
        out_shape=jax.ShapeDtypeStruct((n_rows, c), F32),
        compiler_params=_cparams(("parallel",)),
        name="ssd_conv",
    )(xbc, xbc, xbc, w, b.reshape(1, c))


def _hgrn_kernel(qi_f, f_f, qi_b, f_b, o_f, o_b, st, *, ch):
    @pl.when(pl.program_id(1) == 0)
    def _():
        st[...] = jnp.zeros_like(st)

    L = A_CHUNK
    nsub = ch // L
    ri = _iota((L, L), 0)
    ci = _iota((L, L), 1)
    incl = (ri >= ci, ri <= ci)
    tri = tuple(jnp.where(m, 1.0, 0.0).astype(BF16) for m in incl)
    qis, fs, outs = (qi_f, qi_b), (f_f, f_b), (o_f, o_b)

    def body(j, carry):
        for d in range(2):
            jj = j if d == 0 else nsub - 1 - j
            r0 = pl.multiple_of(jj * L, L)
            for h in range(A_HEADS):
                cs = slice(h * A_DK, (h + 1) * A_DK)
                q = qis[d][pl.ds(r0, L), cs]
                v = qis[d][pl.ds(r0, L), A_DIM + h * A_DK:A_DIM + (h + 1) * A_DK]
                f = fs[d][pl.ds(r0, L), cs]
                k = 1.0 - f
                bc = _mm_exact_l(tri[d], jnp.log(f))
                if d == 0:
                    blast, bref = bc[L - 1:L, :], bc[L // 2 - 1:L // 2, :]
                else:
                    blast, bref = bc[0:1, :], bc[L // 2:L // 2 + 1, :]
                s_prev = st[d, h]
                ds = _mm_tn(v, k * jnp.exp(blast - bc))
                sc = _mm_nt(q * jnp.exp(bc - bref), k * jnp.exp(bref - bc))
                sc = jnp.where(incl[d], sc, 0.0)
                o = _mm(sc, v) + _mm_nt(q * jnp.exp(bc), s_prev)
                st[d, h] = jnp.exp(blast) * s_prev + ds
                outs[d][pl.ds(r0, L), cs] = o
        return carry

    lax.fori_loop(0, nsub, body, 0)


def _hgrn_scan(qi, f, *, n_rows, t_lat, t_ctx, bsz):
    ch = 256
    nlat, nctx = t_lat // ch, t_ctx // ch
    fwd = functools.partial(_scan_block, nlat=nlat, nctx=nctx, bsz=bsz, reverse=False)
    bwd = functools.partial(_scan_block, nlat=nlat, nctx=nctx, bsz=bsz, reverse=True)
    return pl.pallas_call(
        functools.partial(_hgrn_kernel, ch=ch),
        grid=(bsz, nlat + nctx),
        in_specs=[pl.BlockSpec((ch, 2 * A_DIM), lambda b, s: (fwd(b, s), 0)),
                  pl.BlockSpec((ch, A_DIM), lambda b, s: (fwd(b, s), 0)),
                  pl.BlockSpec((ch, 2 * A_DIM), lambda b, s: (bwd(b, s), 0)),
                  pl.BlockSpec((ch, A_DIM), lambda b, s: (bwd(b, s), 1))],
        out_specs=[pl.BlockSpec((ch, A_DIM), lambda b, s: (fwd(b, s), 0)),
                   pl.BlockSpec((ch, A_DIM), lambda b, s: (bwd(b, s), 0))],
        out_shape=[jax.ShapeDtypeStruct((n_rows, A_DIM), F32)] * 2,
        scratch_shapes=[pltpu.VMEM((2, A_HEADS, A_DK, A_DK), F32)],
        compiler_params=_cparams(("parallel", "arbitrary")),
        name="hgrn_scan",
    )(qi, f, qi, f)


def _ssd_kernel(x_f, dt_f, x_b, dt_b, a_ref, dsk_ref, e_ref, o_f, o_b, st):
    @pl.when(pl.program_id(1) == 0)
    def _():
        st[...] = jnp.zeros_like(st)

    L = B_CHUNK
    hg = B_HEADS // B_GROUPS
    gw = hg * B_HEADDIM
    ri = _iota((L, L), 0)
    ci = _iota((L, L), 1)
    incl = (ri >= ci, ri <= ci)
    xs, dts, outs = (x_f, x_b), (dt_f, dt_b), (o_f, o_b)
    for d in range(2):
        tri = jnp.where(incl[d], 1.0, 0.0).astype(BF16)
        xa = xs[d][:, 0:B_DIM]
        dt = dts[d][...]
        acum = _mm_exact_l(tri, dt * a_ref[...])
        e_d = e_ref[d]
        acum_e = _mm_exact_r(acum, e_d)
        dt_e = _mm_exact_r(dt, e_d)
        alast_e = acum_e[L - 1:L, :] if d == 0 else acum_e[0:1, :]
        acum_t = acum.T
        xdt = xa * dt_e
        xdec = xdt * jnp.exp(alast_e - acum_e)
        for g in range(B_GROUPS):
            bg = xs[d][:, B_DIM + g * B_STATE:B_DIM + (g + 1) * B_STATE]
            cg = xs[d][:, B_DIM + (B_GROUPS + g) * B_STATE:B_DIM + (B_GROUPS + g + 1) * B_STATE]
            cb = _mm_nt(cg, bg)
            gs = slice(g * gw, (g + 1) * gw)
            s_prev = st[d, g]
            y_off = _mm(cg, s_prev) * jnp.exp(acum_e[:, gs])
            st[d, g] = s_prev * jnp.exp(alast_e[:, gs]) + _mm_tn(bg, xdec[:, gs])
            for hh in range(hg):
                h = g * hg + hh
                j = d * B_HEADS + h
                seg = acum[:, j:j + 1] - acum_t[j:j + 1, :]
                lm = jnp.exp(jnp.where(incl[d], seg, -1e30))
                hs = slice(h * B_HEADDIM, (h + 1) * B_HEADDIM)
                y = _mm(cb * lm, xdt[:, hs]) + y_off[:, hh * B_HEADDIM:(hh + 1) * B_HEADDIM]
                if d == 0:
                    y = y + dsk_ref[:, hs] * xa[:, hs]
                outs[d][:, hs] = y


def _ssd_scan(xbc, dt, a_row, dsk_row, e_mat, *, n_rows, t_lat, t_ctx, bsz):
    ch = B_CHUNK
    nlat, nctx = t_lat // ch, t_ctx // ch
    fwd = functools.partial(_scan_block, nlat=nlat, nctx=nctx, bsz=bsz, reverse=False)
    bwd = functools.partial(_scan_block, nlat=nlat, nctx=nctx, bsz=bsz, reverse=True)
    const2 = lambda b, s: (0, 0)
    return pl.pallas_call(
        _ssd_kernel,
        grid=(bsz, nlat + nctx),
        in_specs=[pl.BlockSpec((ch, B_XBC), lambda b, s: (fwd(b, s), 0)),
                  pl.BlockSpec((ch, LANES), lambda b, s: (fwd(b, s), 0)),
                  pl.BlockSpec((ch, B_XBC), lambda b, s: (bwd(b, s), 0)),
                  pl.BlockSpec((ch, LANES), lambda b, s: (bwd(b, s), 0)),
                  pl.BlockSpec((1, LANES), const2),
                  pl.BlockSpec((1, B_DIM), const2),
                  pl.BlockSpec((2, LANES, B_DIM), lambda b, s: (0, 0, 0))],
        out_specs=[pl.BlockSpec((ch, B_DIM), lambda b, s: (fwd(b, s), 0)),
                   pl.BlockSpec((ch, B_DIM), lambda b, s: (bwd(b, s), 0))],
        out_shape=[jax.ShapeDtypeStruct((n_rows, B_DIM), F32)] * 2,
        scratch_shapes=[pltpu.VMEM((2, B_GROUPS, B_STATE, (B_HEADS // B_GROUPS) * B_HEADDIM), F32)],
        compiler_params=_cparams(("parallel", "arbitrary")),
        name="ssd_scan",
    )(xbc, dt, xbc, dt, a_row, dsk_row, e_mat)


def _even_out_kernel(x_ref, m_ref, oaf, oab, ybf, ybb, gz_ref, ga_ref, gb_ref, w_ref, o_ref):
    oa = oaf[...] + oab[...]
    parts = []
    for h in range(A_HEADS):
        cs = slice(h * A_DK, (h + 1) * A_DK)
        parts.append(_rms(oa[:, cs]) * ga_ref[...] * gz_ref[:, cs])
    ob = (ybf[...] + ybb[...]) * gz_ref[:, A_DIM:A_DIM + B_DIM]
    gw = B_DIM // B_GROUPS
    for g in range(B_GROUPS):
        cs = slice(g * gw, (g + 1) * gw)
        parts.append(_rms(ob[:, cs]) * gb_ref[:, cs])
    cat = jnp.concatenate(parts, axis=-1).astype(BF16)
    y = _dot(cat, w_ref[...], ((1,), (0,)))
    o_ref[...] = x_ref[...] + m_ref[5:6, :] * y


def _even_out(x, mod, oaf, oab, ybf, ybb, gz, ga, gb, w, *, n_rows, t_lat, bsz):
    d = x.shape[1]
    tm = 512
    while n_rows % tm or t_lat % tm:
        tm //= 2
    row = lambda i: (i, 0)
    const = lambda i: (0, 0)
    return pl.pallas_call(
        _even_out_kernel,
        grid=(n_rows // tm,),
        in_specs=[pl.BlockSpec((tm, d), row),
                  pl.BlockSpec((None, N_ADA, d), lambda i: (_mod_row(i, tm, t_lat, bsz), 0, 0)),
                  pl.BlockSpec((tm, A_DIM), row), pl.BlockSpec((tm, A_DIM), row),
                  pl.BlockSpec((tm, B_DIM), row), pl.BlockSpec((tm, B_DIM), row),
                  pl.BlockSpec((tm, A_DIM + B_DIM), row),
                  pl.BlockSpec((1, A_DK), const), pl.BlockSpec((1, B_DIM), const),
                  pl.BlockSpec((A_DIM + B_DIM, d), const)],
        out_specs=pl.BlockSpec((tm, d), row),
        out_shape=jax.ShapeDtypeStruct((n_rows, d), F32),
        compiler_params=_cparams(("parallel",)),
        name="even_out",
    )(x, mod, oaf, oab, ybf, ybb, gz, ga, gb, w)


def _rw_in_kernel(h_ref, up_ref, dn_ref, mix_ref, vec_ref, wr, wk, wv, w1, w2, a1, a2, g1, g2, bd_ref,
                  r_o, v_o, kk_o, bv_o, lw0_o, lw1_o, k0_o, k1_o, b0_o, b1_o, g_o,
                  *, tm, t_lat, t_ctx, n_lat):
    d = D_MODEL
    r0 = pl.program_id(0) * tm
    in_lat, pos, slen = _seq_pos(r0, t_lat, t_ctx, n_lat)
    first = pos == 0
    last = pos + tm == slen
    row = _iota((tm, 1), 0)
    col = row % GRID_W
    ok_prev = jnp.where(in_lat, col, row) != 0
    ok_next = jnp.where(in_lat, col - (GRID_W - 1), row - (tm - 1)) != 0
    q4 = d // 4

    def prev_of(x):
        return jnp.where(ok_prev, pltpu.roll(x, 1, 0), 0.0)

    def next_of(x):
        return jnp.where(ok_next, pltpu.roll(x, tm - 1, 0), 0.0)

    h = h_ref[...]
    h2, h3 = h[:, 2 * q4:3 * q4], h[:, 3 * q4:]
    up = jnp.concatenate([jnp.where(first, 0.0, up_ref[:, 2 * q4:3 * q4]), h2[:tm - GRID_W]], axis=0)
    down = jnp.concatenate([h3[GRID_W:], jnp.where(last, 0.0, dn_ref[:, 3 * q4:])], axis=0)
    hs = jnp.concatenate([prev_of(h[:, :q4]), next_of(h[:, q4:2 * q4]),
                          jnp.where(in_lat, up, prev_of(h2)),
                          jnp.where(in_lat, down, next_of(h3))], axis=1)
    xx = hs - h
    xr, xw, xk, xv, xa, xg = [(h + xx * mix_ref[j:j + 1, :]).astype(BF16) for j in range(6)]
    dn = ((1,), (0,))
    r = _dot(xr, wr[...], dn)
    k = _dot(xk, wk[...], dn)
    v = _dot(xv, wv[...], dn)
    g = _mm(_sigmoid(_dot(xg, g1[...], dn)), g2[...])
    tw = jnp.tanh(_dot(xw, w1[...], dn))
    ta = _dot(xa, a1[...], dn)
    w0, a0 = vec_ref[0:2, :], vec_ref[2:4, :]
    k_k, k_a, r_k = vec_ref[4:5, :], vec_ref[5:6, :], vec_ref[6:7, :]
    bd = bd_ref[...]

    lws, ks, bs = [], [], []
    kkr = k * k_k
    for p in range(C_PAIRS):
        cs = slice(p * LANES, (p + 1) * LANES)
        ss = _mm_exact_r(kkr[:, cs] * kkr[:, cs], bd)
        kk_o[p] = kkr[:, cs] * lax.rsqrt(jnp.maximum(ss, 1e-24))
    for dd in range(2):
        ls = slice(dd * 64, (dd + 1) * 64)
        zw = w0[dd:dd + 1, :] + _mm(tw[:, ls], w2[dd])
        lws.append(-np.float32(np.exp(-0.5)) * _sigmoid(zw))
        asig = _sigmoid(a0[dd:dd + 1, :] + _mm(ta[:, ls], a2[dd]))
        ks.append(k * (1.0 + (asig - 1.0) * k_a))
        bs.append(asig)
    rk = r * (ks[0] + ks[1]) * r_k
    g_o[...] = g
    for p in range(C_PAIRS):
        cs = slice(p * LANES, (p + 1) * LANES)
        r_o[p] = r[:, cs]
        v_o[p] = v[:, cs]
        bv_o[p] = _mm_exact_r(rk[:, cs], bd) * v[:, cs]
        lw0_o[p] = lws[0][:, cs]
        lw1_o[p] = lws[1][:, cs]
        k0_o[p] = ks[0][:, cs]
        k1_o[p] = ks[1][:, cs]
        kk = kk_o[p]
        b0_o[p] = kk * bs[0][:, cs]
        b1_o[p] = kk * bs[1][:, cs]


def _rw_in(h, mix, vecs, wr, wk, wv, w1, w2, a1, a2, g1, g2, bd, *, n_rows, t_lat, t_ctx, n_lat):
    d = h.shape[1]
    tm = 256
    assert t_ctx == tm and t_lat % tm == 0 and tm % GRID_W == 0
    nb64 = n_rows // GRID_W
    c2 = lambda i: (0, 0)
    c3 = lambda i: (0, 0, 0)
    pm_spec = pl.BlockSpec((C_PAIRS, tm, LANES), lambda i: (0, i, 0))
    pm_shape = jax.ShapeDtypeStruct((C_PAIRS, n_rows, LANES), F32)
    return pl.pallas_call(
        functools.partial(_rw_in_kernel, tm=tm, t_lat=t_lat, t_ctx=t_ctx, n_lat=n_lat),
        grid=(n_rows // tm,),
        in_specs=[pl.BlockSpec((tm, d), lambda i: (i, 0)),
                  pl.BlockSpec((GRID_W, d), lambda i: (jnp.maximum(i * (tm // GRID_W) - 1, 0), 0)),
                  pl.BlockSpec((GRID_W, d), lambda i: (jnp.minimum((i + 1) * (tm // GRID_W), nb64 - 1), 0)),
                  pl.BlockSpec((6, d), c2), pl.BlockSpec((8, d), c2),
                  pl.BlockSpec((d, d), c2), pl.BlockSpec((d, d), c2), pl.BlockSpec((d, d), c2),
                  pl.BlockSpec((d, LANES), c2), pl.BlockSpec((2, 64, d), c3),
                  pl.BlockSpec((d, LANES), c2), pl.BlockSpec((2, 64, d), c3),
                  pl.BlockSpec((d, LANES), c2), pl.BlockSpec((LANES, d), c2),
                  pl.BlockSpec((LANES, LANES), c2)],
        out_specs=[pm_spec] * 10 + [pl.BlockSpec((tm, d), lambda i: (i, 0))],
        out_shape=[pm_shape] * 10 + [jax.ShapeDtypeStruct((n_rows, d), F32)],
        compiler_params=_cparams(("parallel",)),
        name="rw_in",
    )(h, h, h, mix, vecs, wr, wk, wv, w1, w2, a1, a2, g1, g2, bd)


def _tri_inverse(n, eye):
    p = eye + n
    q = n
    for _ in range(5):
        q = _mm_hp(q, q)
        p = p + _mm_hp(p, q)
    return p


def _rwkv_kernel(*refs):
    ins_f, ins_b = refs[0:6], refs[6:12]
    y_f, y_b, st = refs[12:]

    @pl.when(pl.program_id(1) == 0)
    def _():
        st[...] = jnp.zeros_like(st)

    L = C_CHUNK
    ri = _iota((L, L), 0)
    ci = _iota((L, L), 1)
    tri = (jnp.where(ri >= ci, 1.0, 0.0).astype(BF16), jnp.where(ri <= ci, 1.0, 0.0).astype(BF16))
    si = _iota((2 * L, 2 * L), 0) % L
    sj = _iota((2 * L, 2 * L), 1) % L
    incl = (si >= sj, si <= sj)
    strict = (si > sj, si < sj)
    eye = jnp.where(_iota((2 * L, 2 * L), 0) == _iota((2 * L, 2 * L), 1), 1.0, 0.0)
    lo_lane = _iota((L, LANES), 1) < C_HEAD

    def stack(x):
        return jnp.concatenate([jnp.where(lo_lane, x, 0.0), jnp.where(lo_lane, 0.0, x)], axis=0)

    def pair(p, carry):
        for d, (ins, y_ref) in enumerate(((ins_f, y_f), (ins_b, y_b))):
            r_ref, lw_ref, k_ref, v_ref, kk_ref, b_ref = ins
            r, lw, k, v, b = r_ref[p], lw_ref[p], k_ref[p], v_ref[p], b_ref[p]
            a = -kk_ref[p]
            c = _mm_exact_l(tri[d], lw)
            cex = c - lw
            if d == 0:
                mid, c_end = c[L // 2 - 1:L // 2, :], c[L - 1:L, :]
            else:
                mid, c_end = c[L // 2:L // 2 + 1, :], c[0:1, :]
            e_k = jnp.exp(mid - c)
            e_end = jnp.exp(c_end - c)
            xa = stack(a * jnp.exp(cex - mid))
            xr = stack(r * jnp.exp(c - mid))
            xb = stack(b * e_k)
            xk = stack(k * e_k)
            xa0 = stack(a * jnp.exp(cex))
            xr0 = stack(r * jnp.exp(c))
            xbe = stack(b * e_end)
            xke = stack(k * e_end)
            vs = stack(v)
            s_prev = st[d, p]
            big = _mm_nt(jnp.concatenate([xa, xr], axis=0), jnp.concatenate([xb, xk], axis=0))
            n_ab = jnp.where(strict[d], big[:2 * L, :2 * L], 0.0)
            n_ak = jnp.where(strict[d], big[:2 * L, 2 * L:], 0.0)
            m_rb = jnp.where(incl[d], big[2 * L:, :2 * L], 0.0)
            m_rk = jnp.where(incl[d], big[2 * L:, 2 * L:], 0.0)
            t_inv = _tri_inverse(n_ab, eye)
            sa = _mm_nt(jnp.concatenate([xa0, xr0], axis=0), s_prev)
            u = _mm_hp(t_inv, sa[:2 * L] + _mm(n_ak, vs))
            uv = jnp.concatenate([u, vs], axis=0)
            y = sa[2 * L:] + _mm(jnp.concatenate([m_rb, m_rk], axis=1), uv)
            st[d, p] = s_prev * jnp.exp(c_end) + _mm_tn(uv, jnp.concatenate([xbe, xke], axis=0))
            y_ref[p] = y[:L] + y[L:]
        return carry

    lax.fori_loop(0, C_PAIRS, pair, 0)


def _rwkv_scan(r, v, kk, lw, k, b, *, n_rows, t_lat, t_ctx, bsz):
    ch = C_CHUNK
    nlat, nctx = t_lat // ch, t_ctx // ch
    fwd = functools.partial(_scan_block, nlat=nlat, nctx=nctx, bsz=bsz, reverse=False)
    bwd = functools.partial(_scan_block, nlat=nlat, nctx=nctx, bsz=bsz, reverse=True)
    sf = pl.BlockSpec((C_PAIRS, ch, LANES), lambda bb, s: (0, fwd(bb, s), 0))
    sb = pl.BlockSpec((C_PAIRS, ch, LANES), lambda bb, s: (0, bwd(bb, s), 0))
    return pl.pallas_call(
        _rwkv_kernel,
        grid=(bsz, nlat + nctx),
        in_specs=[sf] * 6 + [sb] * 6,
        out_specs=[sf, sb],
        out_shape=[jax.ShapeDtypeStruct((C_PAIRS, n_rows, LANES), F32)] * 2,
        scratch_shapes=[pltpu.VMEM((2, C_PAIRS, LANES, LANES), F32)],
        compiler_params=_cparams(("parallel", "arbitrary")),
        name="rwkv_scan",
    )(r, lw[0], k[0], v, kk, b[0], r, lw[1], k[1], v, kk, b[1])


def _rw_out_kernel(x_ref, m_ref, yf, yb, bv, g_ref, lnw, lnb, bd_ref, w_ref, o_ref):
    bd = bd_ref[...]
    parts = []
    for p in range(C_PAIRS):
        cs = slice(p * LANES, (p + 1) * LANES)
        y = yf[p] + yb[p]
        mu = _mm_exact_r(y, bd) * (1.0 / C_HEAD)
        yc = y - mu
        var = _mm_exact_r(yc * yc, bd) * (1.0 / C_HEAD)
        yn = yc * lax.rsqrt(var + LNX_EPS) * lnw[:, cs] + lnb[:, cs] + bv[p]
        parts.append((yn * g_ref[:, cs]).astype(BF16))
    cat = jnp.concatenate(parts, axis=-1)
    o_ref[...] = x_ref[...] + m_ref[5:6, :] * _dot(cat, w_ref[...], ((1,), (0,)))


def _rw_out(x, mod, yf, yb, bv, g, lnw, lnb, bd, w, *, n_rows, t_lat, bsz):
    d = x.shape[1]
    tm = 256
    row = lambda i: (i, 0)
    const = lambda i: (0, 0)
    pm = pl.BlockSpec((C_PAIRS, tm, LANES), lambda i: (0, i, 0))
    return pl.pallas_call(
        _rw_out_kernel,
        grid=(n_rows // tm,),
        in_specs=[pl.BlockSpec((tm, d), row),
                  pl.BlockSpec((None, N_ADA, d), lambda i: (_mod_row(i, tm, t_lat, bsz), 0, 0)),
                  pm, pm, pm, pl.BlockSpec((tm, d), row),
                  pl.BlockSpec((1, d), const), pl.BlockSpec((1, d), const),
                  pl.BlockSpec((LANES, LANES), const), pl.BlockSpec((d, d), const)],
        out_specs=pl.BlockSpec((tm, d), row),
        out_shape=jax.ShapeDtypeStruct((n_rows, d), F32),
        compiler_params=_cparams(("parallel",)),
        name="rw_out",
    )(x, mod, yf, yb, bv, g, lnw.reshape(1, d), lnb.reshape(1, d), bd, w)


def _even_layer(x, mod, g, wi, wo, in_w, out_w, lb, hgrn_g, conv_w, conv_b, dt_bias, a_log, d_skip, ssd_g, dims):
    n_all, t_lat, t_ctx, bsz = dims["n_all"], dims["t_lat"], dims["t_ctx"], dims["bsz"]
    n_lat = bsz * t_lat
    x, h = _ffn_half(x, mod, g[0], wi[0], wo[0], slot=0, n_rows=n_all, t_lat=t_lat, bsz=bsz,
                     post="prenorm", g2=g[1], h_dtype=BF16)
    w_pad = jnp.pad(in_w, ((0, 0), (0, EVEN_IN_PAD - EVEN_IN))).astype(BF16)
    dtb = jnp.pad(dt_bias.reshape(1, 2 * B_HEADS), ((0, 0), (0, LANES - 2 * B_HEADS)))
    qi, f, gz, xbc, dt = _even_in(h, w_pad, lb.reshape(1, 2 * A_DIM), dtb, n_all)
    xbc = _conv_silu(xbc, conv_w, conv_b, n_rows=n_all, t_lat=t_lat, t_ctx=t_ctx, n_lat=n_lat)
    oaf, oab = _hgrn_scan(qi, f, n_rows=n_all, t_lat=t_lat, t_ctx=t_ctx, bsz=bsz)
    a_row = jnp.pad(-jnp.exp(a_log.astype(F32)).reshape(1, 2 * B_HEADS), ((0, 0), (0, LANES - 2 * B_HEADS)))
    dsk_row = jnp.repeat(d_skip.astype(F32), B_HEADDIM).reshape(1, B_DIM)
    e_np = np.zeros((2, LANES, B_DIM), np.float32)
    for dd in range(2):
        for hh in range(B_HEADS):
            e_np[dd, dd * B_HEADS + hh, hh * B_HEADDIM:(hh + 1) * B_HEADDIM] = 1.0
    ybf, ybb = _ssd_scan(xbc, dt, a_row, dsk_row, jnp.asarray(e_np, BF16),
                         n_rows=n_all, t_lat=t_lat, t_ctx=t_ctx, bsz=bsz)
    x = _even_out(x, mod, oaf, oab, ybf, ybb, gz, hgrn_g.reshape(1, A_DK), ssd_g.reshape(1, B_DIM),
                  out_w.astype(BF16), n_rows=n_all, t_lat=t_lat, bsz=bsz)
    return x


def _odd_layer(x, mod, g, wi, wo, p, dims):
    n_all, t_lat, t_ctx, bsz = dims["n_all"], dims["t_lat"], dims["t_ctx"], dims["bsz"]
    n_lat = bsz * t_lat
    d = D_MODEL
    x, h = _ffn_half(x, mod, g[0], wi[0], wo[0], slot=0, n_rows=n_all, t_lat=t_lat, bsz=bsz,
                     post="prenorm", g2=g[1], h_dtype=F32)
    vecs = jnp.concatenate([p["w0"], p["a0"], p["k_k"][None], p["k_a"][None], p["r_k"].reshape(1, d),
                            jnp.zeros((1, d), F32)], axis=0)
    w1 = jnp.concatenate([p["w1"][0], p["w1"][1]], axis=1).astype(BF16)
    a1 = jnp.concatenate([p["a1"][0], p["a1"][1]], axis=1).astype(BF16)
    bd_np = np.kron(np.eye(2, dtype=np.float32), np.ones((C_HEAD, C_HEAD), np.float32))
    bd = jnp.asarray(bd_np, BF16)
    outs = _rw_in(h, p["x_mix"], vecs, p["wr"].astype(BF16), p["wk"].astype(BF16), p["wv"].astype(BF16),
                  w1, p["w2"].astype(BF16), a1, p["a2"].astype(BF16), p["g1"].astype(BF16),
                  p["g2"].astype(BF16), bd, n_rows=n_all, t_lat=t_lat, t_ctx=t_ctx, n_lat=n_lat)
    r, v, kk, bv, lw0, lw1, k0, k1, b0, b1, gg = outs
    yf, yb = _rwkv_scan(r, v, kk, (lw0, lw1), (k0, k1), (b0, b1), n_rows=n_all, t_lat=t_lat, t_ctx=t_ctx, bsz=bsz)
    x = _rw_out(x, mod, yf, yb, bv, gg, p["ln_w"], p["ln_b"], bd, p["wo"].astype(BF16),
                n_rows=n_all, t_lat=t_lat, bsz=bsz)
    return x


def kernel(x, c, ctx, c_ctx, ada_w, ada_b, norm_g, ffn_wi, ffn_wo, final_g, hgrn_lb_logits, ev_in_w, ev_out_w, hgrn_norm_g, ssd_conv_w, ssd_conv_b, ssd_dt_bias, ssd_a_log, ssd_d, ssd_norm_g, rw_x_mix, rw_wr, rw_wk, rw_wv, rw_wo, rw_w0, rw_w1, rw_w2, rw_a0, rw_a1, rw_a2, rw_g1, rw_g2, rw_k_k, rw_k_a, rw_r_k, rw_ln_w, rw_ln_b):
    bsz, t_lat, d = x.shape
    t_ctx = ctx.shape[1]
    depth = ada_w.shape[0]
    n_lat = bsz * t_lat
    n_all = n_lat + bsz * t_ctx
    dims = dict(n_all=n_all, t_lat=t_lat, t_ctx=t_ctx, bsz=bsz)
    assert bsz + 1 <= 8

    xs = jnp.concatenate([x.reshape(n_lat, d), ctx.reshape(bsz * t_ctx, d)], axis=0)
    cvec = jnp.concatenate([c, c_ctx[None], jnp.zeros((8 - bsz - 1, d), F32)], axis=0)
    mod_all = _modulation(cvec, ada_w, ada_b).reshape(depth, 8, N_ADA, d)
    lb_all = jnp.cumsum(jax.nn.softmax(hgrn_lb_logits.astype(F32), axis=0), axis=0)
    wi_bf = ffn_wi.astype(BF16)
    wo_bf = ffn_wo.astype(BF16)

    for layer in range(depth):
        last = layer == depth - 1
        mod = mod_all[layer]
        g = norm_g[layer]
        if layer % 2 == 0:
            e = layer // 2
            xs = _even_layer(xs, mod, g, wi_bf[layer], wo_bf[layer], ev_in_w[e], ev_out_w[e], lb_all[layer],
                             hgrn_norm_g[e], ssd_conv_w[e], ssd_conv_b[e], ssd_dt_bias[e], ssd_a_log[e],
                             ssd_d[e], ssd_norm_g[e], dims)
        else:
            o = layer // 2
            p = dict(x_mix=rw_x_mix[o], wr=rw_wr[o], wk=rw_wk[o], wv=rw_wv[o], wo=rw_wo[o], w0=rw_w0[o],
                     w1=rw_w1[o], w2=rw_w2[o], a0=rw_a0[o], a1=rw_a1[o], a2=rw_a2[o], g1=rw_g1[o], g2=rw_g2[o],
                     k_k=rw_k_k[o], k_a=rw_k_a[o], r_k=rw_r_k[o], ln_w=rw_ln_w[o], ln_b=rw_ln_b[o])
            xs = _odd_layer(xs, mod, g, wi_bf[layer], wo_bf[layer], p, dims)
        if last:
            out = _ffn_half(xs, mod, g[2], wi_bf[layer, 1], wo_bf[layer, 1], slot=2, n_rows=n_lat, t_lat=t_lat,
                            bsz=bsz, post="final", g2=final_g)
        else:
            xs = _ffn_half(xs, mod, g[2], wi_bf[layer, 1], wo_bf[layer, 1], slot=2, n_rows=n_all, t_lat=t_lat,
                           bsz=bsz)
    return out.reshape(bsz, t_lat, d)
```

```python
import functools

import numpy as np
import jax
import jax.numpy as jnp
from jax import lax
from jax.experimental import pallas as pl
from jax.experimental.pallas import tpu as pltpu

F32 = jnp.float32
BF16 = jnp.bfloat16

D_MODEL = 1024
N_ADA = 9
FF_DIM = 2816
EPS = 1e-6
GRID_W = 64
A_HEADS = 4
A_DK = 128
A_DIM = A_HEADS * A_DK
A_CHUNK = 32
B_HEADS = 8
B_HEADDIM = 64
B_DIM = B_HEADS * B_HEADDIM
B_GROUPS = 2
B_STATE = 128
B_CHUNK = 128
B_XBC = B_DIM + 2 * B_GROUPS * B_STATE
EVEN_IN = A_DIM * 5 + B_DIM + B_XBC + 2 * B_HEADS
EVEN_IN_PAD = 4224
C_HEAD = 64
C_HEADS = D_MODEL // C_HEAD
C_PAIRS = C_HEADS // 2
C_CHUNK = 64
LNX_EPS = 64e-5
RW_GROUP = 8
LANES = 128

VMEM_LIMIT = 56 * 1024 * 1024


def _cparams(sem):
    return pltpu.CompilerParams(dimension_semantics=sem, vmem_limit_bytes=VMEM_LIMIT)


def _sigmoid(x):
    return 1.0 / (1.0 + jnp.exp(-x))


def _silu(x):
    return x * _sigmoid(x)


def _softplus(x):
    return jnp.maximum(x, 0.0) + jnp.log1p(jnp.exp(-jnp.abs(x)))


def _dot(a, b, dims):
    return lax.dot_general(a, b, (dims, ((), ())), preferred_element_type=F32)


def _mm(a, b):
    return _dot(a.astype(BF16), b.astype(BF16), ((1,), (0,)))


def _mm_nt(a, b):
    return _dot(a.astype(BF16), b.astype(BF16), ((1,), (1,)))


def _mm_tn(a, b):
    return _dot(a.astype(BF16), b.astype(BF16), ((0,), (0,)))


def _split2(x):
    hi = x.astype(BF16)
    lo = (x - hi.astype(F32)).astype(BF16)
    return hi, lo


def _split3(x):
    x1 = x.astype(BF16)
    r1 = x - x1.astype(F32)
    x2 = r1.astype(BF16)
    x3 = (r1 - x2.astype(F32)).astype(BF16)
    return x1, x2, x3


def _mm_exact_l(m_bf, x):
    x1, x2, x3 = _split3(x)
    dn = ((1,), (0,))
    return _dot(m_bf, x1, dn) + _dot(m_bf, x2, dn) + _dot(m_bf, x3, dn)


def _mm_exact_r(x, m_bf):
    x1, x2, x3 = _split3(x)
    dn = ((1,), (0,))
    return _dot(x1, m_bf, dn) + _dot(x2, m_bf, dn) + _dot(x3, m_bf, dn)


def _mm_hp(a, b):
    ah, al = _split2(a)
    bh, bl = _split2(b)
    dn = ((1,), (0,))
    return _dot(ah, bh, dn) + _dot(ah, bl, dn) + _dot(al, bh, dn)


def _rms(x):
    return x * lax.rsqrt(jnp.mean(x * x, axis=-1, keepdims=True) + EPS)


def _iota(shape, dim):
    return lax.broadcasted_iota(jnp.int32, shape, dim)


def _mod_row(i, tm, t_lat, bsz):
    return jnp.minimum((i * tm) // t_lat, bsz)


def _seq_pos(r0, t_lat, t_ctx, n_lat):
    in_lat = r0 < n_lat
    pos = jnp.where(in_lat, r0 % t_lat, (r0 - n_lat) % t_ctx)
    slen = jnp.where(in_lat, t_lat, t_ctx)
    return in_lat, pos, slen


def _scan_block(b, s, nlat, nctx, bsz, reverse):
    if reverse:
        cblk = bsz * nlat + b * nctx + (nctx - 1 - s)
        lblk = b * nlat + (nlat - 1 - (s - nctx))
    else:
        cblk = bsz * nlat + b * nctx + s
        lblk = b * nlat + (s - nctx)
    return jnp.where(s < nctx, cblk, lblk)


def _mod_kernel(c_ref, w_ref, b_ref, o_ref):
    sc = _silu(c_ref[...])
    o_ref[...] = _mm(sc, w_ref[...]) + b_ref[...]


def _modulation(cvec, ada_w, ada_b):
    depth, d, n = ada_w.shape
    tn = 1536
    return pl.pallas_call(
        _mod_kernel,
        grid=(depth, n // tn),
        in_specs=[pl.BlockSpec((8, d), lambda l, j: (0, 0)),
                  pl.BlockSpec((None, d, tn), lambda l, j: (l, 0, j)),
                  pl.BlockSpec((None, 1, tn), lambda l, j: (l, 0, j))],
        out_specs=pl.BlockSpec((None, 8, tn), lambda l, j: (l, 0, j)),
        out_shape=jax.ShapeDtypeStruct((depth, 8, n), F32),
        compiler_params=_cparams(("arbitrary", "arbitrary")),
        name="ada_mod",
    )(cvec, ada_w, ada_b.reshape(depth, 1, n))


def _ffn_kernel(*refs, slot, nf, post):
    x_ref, m_ref, g_ref, wg_ref, wu_ref, wo_ref = refs[:6]
    if post == "prenorm":
        g2_ref, o_ref, h_ref, hs, acc = refs[6:]
    elif post == "final":
        g2_ref, o_ref, hs, acc = refs[6:]
    else:
        o_ref, hs, acc = refs[6:]
    f = pl.program_id(1)

    @pl.when(f == 0)
    def _():
        x = x_ref[...]
        h = _rms(x) * g_ref[...] * (1.0 + m_ref[3 * slot + 1:3 * slot + 2, :]) + m_ref[3 * slot:3 * slot + 1, :]
        hs[...] = h.astype(BF16)
        acc[...] = jnp.zeros_like(acc)

    h = hs[...]
    gt = _dot(h, wg_ref[...], ((1,), (0,)))
    up = _dot(h, wu_ref[...], ((1,), (0,)))
    act = (_silu(gt) * up).astype(BF16)
    acc[...] += _dot(act, wo_ref[...], ((1,), (0,)))

    @pl.when(f == nf - 1)
    def _():
        xn = x_ref[...] + 0.5 * m_ref[3 * slot + 2:3 * slot + 3, :] * acc[...]
        if post == "final":
            o_ref[...] = _rms(xn) * g2_ref[...]
        else:
            o_ref[...] = xn
        if post == "prenorm":
            h2 = _rms(xn) * g2_ref[...] * (1.0 + m_ref[4:5, :]) + m_ref[3:4, :]
            h_ref[...] = h2.astype(h_ref.dtype)


def _ffn_half(x, mod, g, wi, wo, *, slot, n_rows, t_lat, bsz, post=None, g2=None, h_dtype=F32):
    d = x.shape[1]
    tm = 1024
    while n_rows % tm or t_lat % tm:
        tm //= 2
    tf = 256
    nf = FF_DIM // tf
    in_specs = [pl.BlockSpec((tm, d), lambda i, f: (i, 0)),
                pl.BlockSpec((None, N_ADA, d), lambda i, f: (_mod_row(i, tm, t_lat, bsz), 0, 0)),
                pl.BlockSpec((1, d), lambda i, f: (0, 0)),
                pl.BlockSpec((d, tf), lambda i, f: (0, f)),
                pl.BlockSpec((d, tf), lambda i, f: (0, nf + f)),
                pl.BlockSpec((tf, d), lambda i, f: (f, 0))]
    args = [x, mod, g.reshape(1, d), wi, wi, wo]
    out_spec = pl.BlockSpec((tm, d), lambda i, f: (i, 0))
    out_shape = jax.ShapeDtypeStruct((n_rows, d), F32)
    if post is not None:
        in_specs.append(pl.BlockSpec((1, d), lambda i, f: (0, 0)))
        args.append(g2.reshape(1, d))
    if post == "prenorm":
        out_specs = [out_spec, out_spec]
        out_shapes = [out_shape, jax.ShapeDtypeStruct((n_rows, d), h_dtype)]
    else:
        out_specs, out_shapes = out_spec, out_shape
    return pl.pallas_call(
        functools.partial(_ffn_kernel, slot=slot, nf=nf, post=post),
        grid=(n_rows // tm, nf),
        in_specs=in_specs,
        out_specs=out_specs,
        out_shape=out_shapes,
        scratch_shapes=[pltpu.VMEM((tm, d), BF16), pltpu.VMEM((tm, d), F32)],
        compiler_params=_cparams(("parallel", "arbitrary")),
        name="ffn_half",
    )(*args)


def _even_in_kernel(h_ref, w_ref, lb_ref, dtb_ref, qi_ref, f_ref, gz_ref, xbc_ref, dt_ref):
    h = h_ref[...]
    dn = ((1,), (0,))
    qi = _dot(h, w_ref[:, 0:1024], dn)
    qi_ref[:, 0:512] = _silu(qi[:, 0:512])
    qi_ref[:, 512:1024] = qi[:, 512:1024]
    lb = lb_ref[...]
    f_ref[...] = lb + (1.0 - lb) * _sigmoid(_dot(h, w_ref[:, 1024:2048], dn))
    gz_ref[...] = _silu(_dot(h, w_ref[:, 2048:3072], dn))
    xbc_ref[...] = _dot(h, w_ref[:, 3072:4096], dn)
    dt_ref[...] = _softplus(_dot(h, w_ref[:, 4096:4224], dn) + dtb_ref[...])


def _even_in(h, w, lb, dtb, n_rows):
    d = h.shape[1]
    tm = 512
    while n_rows % tm:
        tm //= 2
    widths = (1024, 1024, 1024, 1024, LANES)
    return pl.pallas_call(
        _even_in_kernel,
        grid=(n_rows // tm,),
        in_specs=[pl.BlockSpec((tm, d), lambda i: (i, 0)),
                  pl.BlockSpec((d, EVEN_IN_PAD), lambda i: (0, 0)),
                  pl.BlockSpec((1, 1024), lambda i: (0, 0)),
                  pl.BlockSpec((1, LANES), lambda i: (0, 0))],
        out_specs=[pl.BlockSpec((tm, wd), lambda i: (i, 0)) for wd in widths],
        out_shape=[jax.ShapeDtypeStruct((n_rows, wd), F32) for wd in widths],
        compiler_params=_cparams(("parallel",)),
        name="even_in",
    )(h, w, lb, dtb)


def _conv_kernel(x_ref, p_ref, n_ref, w_ref, b_ref, o_ref, *, tm, t_lat, t_ctx, n_lat):
    r0 = pl.program_id(0) * tm
    _, pos, slen = _seq_pos(r0, t_lat, t_ctx, n_lat)
    first = pos == 0
    last = pos + tm == slen
    x = x_ref[...]
    row = _iota((tm, 1), 0)
    prev_row = jnp.where(first, 0.0, p_ref[7:8, :])
    next_row = jnp.where(last, 0.0, n_ref[0:1, :])
    xp = jnp.where(row == 0, prev_row, pltpu.roll(x, 1, 0))
    xn = jnp.where(row == tm - 1, next_row, pltpu.roll(x, tm - 1, 0))
    y = w_ref[0:1, :] * xp + w_ref[1:2, :] * x + w_ref[2:3, :] * xn + b_ref[...]
    o_ref[...] = _silu(y)


def _conv_silu(xbc, w, b, *, n_rows, t_lat, t_ctx, n_lat):
    c = xbc.shape[1]
    tm = 256
    nb8 = n_rows // 8
    return pl.pallas_call(
        functools.partial(_conv_kernel, tm=tm, t_lat=t_lat, t_ctx=t_ctx, n_lat=n_lat),
        grid=(n_rows // tm,),
        in_specs=[pl.BlockSpec((tm, c), lambda i: (i, 0)),
                  pl.BlockSpec((8, c), lambda i: (jnp.maximum(i * (tm // 8) - 1, 0), 0)),
                  pl.BlockSpec((8, c), lambda i: (jnp.minimum((i + 1) * (tm // 8), nb8 - 1), 0)),
                  pl.BlockSpec((3, c), lambda i: (0, 0)),
                  pl.BlockSpec((1, c), lambda i: (0, 0))],
        out_specs=pl.BlockSpec((tm, c), lambda i: (i, 0)),
        out_shape=jax.ShapeDtypeStruct((n_rows, c), F32),
        compiler_params=_cparams(("parallel",)),
        name="ssd_conv",
    )(xbc, xbc, xbc, w, b.reshape(1, c))


def _hgrn_kernel(qi_f, f_f, qi_b, f_b, o_f, o_b, st, *, ch):
    @pl.when(pl.program_id(1) == 0)
    def _():
        st[...] = jnp.zeros_like(st)

    L = A_CHUNK
    nsub = ch // L
    ri = _iota((L, L), 0)
    ci = _iota((L, L), 1)
    incl = (ri >= ci, ri <= ci)
    tri = tuple(jnp.where(m, 1.0, 0.0).astype(BF16) for m in incl)
    qis, fs, outs = (qi_f, qi_b), (f_f, f_b), (o_f, o_b)

    chains = [(d, h) for h in range(A_HEADS) for d in range(2)]

    def body(j, carry):
        r0s = (pl.multiple_of(j * L, L), pl.multiple_of((nsub - 1 - j) * L, L))
        qs, vs, ks, logfs = [], [], [], []
        for d, h in chains:
            cs = slice(h * A_DK, (h + 1) * A_DK)
            rows = pl.ds(r0s[d], L)
            qs.append(qis[d][rows, cs])
            vs.append(qis[d][rows, A_DIM + h * A_DK:A_DIM + (h + 1) * A_DK].astype(BF16))
            f = fs[d][rows, cs]
            ks.append(1.0 - f)
            logfs.append(jnp.log(f))
        bcs = [_mm_exact_l(tri[d], lf) for (d, _), lf in zip(chains, logfs)]
        s_prev = [st[d, h] for d, h in chains]
        dss, scs, o2s, decs = [], [], [], []
        for i, (d, h) in enumerate(chains):
            bc = bcs[i]
            if d == 0:
                blast, bref = bc[L - 1:L, :], bc[L // 2 - 1:L // 2, :]
            else:
                blast, bref = bc[0:1, :], bc[L // 2:L // 2 + 1, :]
            dss.append(_mm_tn(vs[i], ks[i] * jnp.exp(blast - bc)))
            scs.append(_mm_nt(qs[i] * jnp.exp(bc - bref), ks[i] * jnp.exp(bref - bc)))
            o2s.append(_mm_nt(qs[i] * jnp.exp(bc), s_prev[i]))
            decs.append(jnp.exp(blast))
        o1s = [_mm(jnp.where(incl[d], sc, 0.0), v) for (d, _), sc, v in zip(chains, scs, vs)]
        for i, (d, h) in enumerate(chains):
            st[d, h] = decs[i] * s_prev[i] + dss[i]
            outs[d][pl.ds(r0s[d], L), h * A_DK:(h + 1) * A_DK] = o1s[i] + o2s[i]
        return carry

    lax.fori_loop(0, nsub, body, 0)


def _hgrn_scan(qi, f, *, n_rows, t_lat, t_ctx, bsz):
    ch = 256
    nlat, nctx = t_lat // ch, t_ctx // ch
    fwd = functools.partial(_scan_block, nlat=nlat, nctx=nctx, bsz=bsz, reverse=False)
    bwd = functools.partial(_scan_block, nlat=nlat, nctx=nctx, bsz=bsz, reverse=True)
    return pl.pallas_call(
        functools.partial(_hgrn_kernel, ch=ch),
        grid=(bsz, nlat + nctx),
        in_specs=[pl.BlockSpec((ch, 2 * A_DIM), lambda b, s: (fwd(b, s), 0)),
                  pl.BlockSpec((ch, A_DIM), lambda b, s: (fwd(b, s), 0)),
                  pl.BlockSpec((ch, 2 * A_DIM), lambda b, s: (bwd(b, s), 0)),
                  pl.BlockSpec((ch, A_DIM), lambda b, s: (bwd(b, s), 1))],
        out_specs=[pl.BlockSpec((ch, A_DIM), lambda b, s: (fwd(b, s), 0)),
                   pl.BlockSpec((ch, A_DIM), lambda b, s: (bwd(b, s), 0))],
        out_shape=[jax.ShapeDtypeStruct((n_rows, A_DIM), F32)] * 2,
        scratch_shapes=[pltpu.VMEM((2, A_HEADS, A_DK, A_DK), F32)],
        compiler_params=_cparams(("parallel", "arbitrary")),
        name="hgrn_scan",
    )(qi, f, qi, f)


def _ssd_kernel(x_f, dt_f, x_b, dt_b, a_ref, dsk_ref, e_ref, o_f, o_b, st):
    @pl.when(pl.program_id(1) == 0)
    def _():
        st[...] = jnp.zeros_like(st)

    L = B_CHUNK
    hg = B_HEADS // B_GROUPS
    gw = hg * B_HEADDIM
    ri = _iota((L, L), 0)
    ci = _iota((L, L), 1)
    incl = (ri >= ci, ri <= ci)
    xs, dts, outs = (x_f, x_b), (dt_f, dt_b), (o_f, o_b)
    tris = [jnp.where(incl[d], 1.0, 0.0).astype(BF16) for d in range(2)]
    dt = [dts[d][...] for d in range(2)]
    acum = [_mm_exact_l(tris[d], dt[d] * a_ref[...]) for d in range(2)]
    acum_e = [_mm_exact_r(acum[d], e_ref[d]) for d in range(2)]
    dt_e = [_mm_exact_r(dt[d], e_ref[d]) for d in range(2)]
    alast_e = [acum_e[0][L - 1:L, :], acum_e[1][0:1, :]]
    acum_t = [acum[d].T for d in range(2)]
    xa = [xs[d][:, 0:B_DIM] for d in range(2)]
    xdt = [xa[d] * dt_e[d] for d in range(2)]
    xdec = [xdt[d] * jnp.exp(alast_e[d] - acum_e[d]) for d in range(2)]
    dg = [(d, g) for d in range(2) for g in range(B_GROUPS)]
    bgs = [xs[d][:, B_DIM + g * B_STATE:B_DIM + (g + 1) * B_STATE].astype(BF16) for d, g in dg]
    cgs = [xs[d][:, B_DIM + (B_GROUPS + g) * B_STATE:B_DIM + (B_GROUPS + g + 1) * B_STATE].astype(BF16)
           for d, g in dg]
    cbs = [_mm_nt(cgs[i], bgs[i]) for i in range(len(dg))]
    s_prev = [st[d, g] for d, g in dg]
    y_off = [_mm(cgs[i], s_prev[i]) * jnp.exp(acum_e[d][:, g * gw:(g + 1) * gw]) for i, (d, g) in enumerate(dg)]
    for i, (d, g) in enumerate(dg):
        gs = slice(g * gw, (g + 1) * gw)
        st[d, g] = s_prev[i] * jnp.exp(alast_e[d][:, gs]) + _mm_tn(bgs[i], xdec[d][:, gs])
    ys = []
    for i, (d, g) in enumerate(dg):
        for hh in range(hg):
            h = g * hg + hh
            j = d * B_HEADS + h
            seg = acum[d][:, j:j + 1] - acum_t[d][j:j + 1, :]
            lm = jnp.exp(jnp.where(incl[d], seg, -1e30))
            ys.append(_mm(cbs[i] * lm, xdt[d][:, h * B_HEADDIM:(h + 1) * B_HEADDIM]))
    for i, (d, g) in enumerate(dg):
        for hh in range(hg):
            h = g * hg + hh
            hs = slice(h * B_HEADDIM, (h + 1) * B_HEADDIM)
            y = ys[i * hg + hh] + y_off[i][:, hh * B_HEADDIM:(hh + 1) * B_HEADDIM]
            if d == 0:
                y = y + dsk_ref[:, hs] * xa[d][:, hs]
            outs[d][:, hs] = y


def _ssd_scan(xbc, dt, a_row, dsk_row, e_mat, *, n_rows, t_lat, t_ctx, bsz):
    ch = B_CHUNK
    nlat, nctx = t_lat // ch, t_ctx // ch
    fwd = functools.partial(_scan_block, nlat=nlat, nctx=nctx, bsz=bsz, reverse=False)
    bwd = functools.partial(_scan_block, nlat=nlat, nctx=nctx, bsz=bsz, reverse=True)
    const2 = lambda b, s: (0, 0)
    return pl.pallas_call(
        _ssd_kernel,
        grid=(bsz, nlat + nctx),
        in_specs=[pl.BlockSpec((ch, B_XBC), lambda b, s: (fwd(b, s), 0)),
                  pl.BlockSpec((ch, LANES), lambda b, s: (fwd(b, s), 0)),
                  pl.BlockSpec((ch, B_XBC), lambda b, s: (bwd(b, s), 0)),
                  pl.BlockSpec((ch, LANES), lambda b, s: (bwd(b, s), 0)),
                  pl.BlockSpec((1, LANES), const2),
                  pl.BlockSpec((1, B_DIM), const2),
                  pl.BlockSpec((2, LANES, B_DIM), lambda b, s: (0, 0, 0))],
        out_specs=[pl.BlockSpec((ch, B_DIM), lambda b, s: (fwd(b, s), 0)),
                   pl.BlockSpec((ch, B_DIM), lambda b, s: (bwd(b, s), 0))],
        out_shape=[jax.ShapeDtypeStruct((n_rows, B_DIM), F32)] * 2,
        scratch_shapes=[pltpu.VMEM((2, B_GROUPS, B_STATE, (B_HEADS // B_GROUPS) * B_HEADDIM), F32)],
        compiler_params=_cparams(("parallel", "arbitrary")),
        name="ssd_scan",
    )(xbc, dt, xbc, dt, a_row, dsk_row, e_mat)


def _even_out_kernel(x_ref, m_ref, oaf, oab, ybf, ybb, gz_ref, ga_ref, gb_ref, w_ref, o_ref):
    oa = oaf[...] + oab[...]
    parts = []
    for h in range(A_HEADS):
        cs = slice(h * A_DK, (h + 1) * A_DK)
        parts.append(_rms(oa[:, cs]) * ga_ref[...] * gz_ref[:, cs])
    ob = (ybf[...] + ybb[...]) * gz_ref[:, A_DIM:A_DIM + B_DIM]
    gw = B_DIM // B_GROUPS
    for g in range(B_GROUPS):
        cs = slice(g * gw, (g + 1) * gw)
        parts.append(_rms(ob[:, cs]) * gb_ref[:, cs])
    cat = jnp.concatenate(parts, axis=-1).astype(BF16)
    y = _dot(cat, w_ref[...], ((1,), (0,)))
    o_ref[...] = x_ref[...] + m_ref[5:6, :] * y


def _even_out(x, mod, oaf, oab, ybf, ybb, gz, ga, gb, w, *, n_rows, t_lat, bsz):
    d = x.shape[1]
    tm = 512
    while n_rows % tm or t_lat % tm:
        tm //= 2
    row = lambda i: (i, 0)
    const = lambda i: (0, 0)
    return pl.pallas_call(
        _even_out_kernel,
        grid=(n_rows // tm,),
        in_specs=[pl.BlockSpec((tm, d), row),
                  pl.BlockSpec((None, N_ADA, d), lambda i: (_mod_row(i, tm, t_lat, bsz), 0, 0)),
                  pl.BlockSpec((tm, A_DIM), row), pl.BlockSpec((tm, A_DIM), row),
                  pl.BlockSpec((tm, B_DIM), row), pl.BlockSpec((tm, B_DIM), row),
                  pl.BlockSpec((tm, A_DIM + B_DIM), row),
                  pl.BlockSpec((1, A_DK), const), pl.BlockSpec((1, B_DIM), const),
                  pl.BlockSpec((A_DIM + B_DIM, d), const)],
        out_specs=pl.BlockSpec((tm, d), row),
        out_shape=jax.ShapeDtypeStruct((n_rows, d), F32),
        compiler_params=_cparams(("parallel",)),
        name="even_out",
    )(x, mod, oaf, oab, ybf, ybb, gz, ga, gb, w)


def _rw_in_kernel(h_ref, up_ref, dn_ref, mix_ref, vec_ref, wr, wk, wv, w1, w2, a1, a2, g1, g2, bd_ref,
                  r_o, v_o, kk_o, bv_o, lw0_o, lw1_o, k0_o, k1_o, b0_o, b1_o, g_o,
                  *, tm, t_lat, t_ctx, n_lat):
    d = D_MODEL
    r0 = pl.program_id(0) * tm
    in_lat, pos, slen = _seq_pos(r0, t_lat, t_ctx, n_lat)
    first = pos == 0
    last = pos + tm == slen
    row = _iota((tm, 1), 0)
    col = row % GRID_W
    ok_prev = jnp.where(in_lat, col, row) != 0
    ok_next = jnp.where(in_lat, col - (GRID_W - 1), row - (tm - 1)) != 0
    q4 = d // 4

    def prev_of(x):
        return jnp.where(ok_prev, pltpu.roll(x, 1, 0), 0.0)

    def next_of(x):
        return jnp.where(ok_next, pltpu.roll(x, tm - 1, 0), 0.0)

    h = h_ref[...]
    h2, h3 = h[:, 2 * q4:3 * q4], h[:, 3 * q4:]
    up = jnp.concatenate([jnp.where(first, 0.0, up_ref[:, 2 * q4:3 * q4]), h2[:tm - GRID_W]], axis=0)
    down = jnp.concatenate([h3[GRID_W:], jnp.where(last, 0.0, dn_ref[:, 3 * q4:])], axis=0)
    hs = jnp.concatenate([prev_of(h[:, :q4]), next_of(h[:, q4:2 * q4]),
                          jnp.where(in_lat, up, prev_of(h2)),
                          jnp.where(in_lat, down, next_of(h3))], axis=1)
    xx = hs - h
    xr, xw, xk, xv, xa, xg = [(h + xx * mix_ref[j:j + 1, :]).astype(BF16) for j in range(6)]
    dn = ((1,), (0,))
    r = _dot(xr, wr[...], dn)
    k = _dot(xk, wk[...], dn)
    v = _dot(xv, wv[...], dn)
    g = _mm(_sigmoid(_dot(xg, g1[...], dn)), g2[...])
    tw = jnp.tanh(_dot(xw, w1[...], dn))
    ta = _dot(xa, a1[...], dn)
    w0, a0 = vec_ref[0:2, :], vec_ref[2:4, :]
    k_k, k_a, r_k = vec_ref[4:5, :], vec_ref[5:6, :], vec_ref[6:7, :]
    bd = bd_ref[...]

    lws, ks, bs = [], [], []
    kkr = k * k_k
    for p in range(C_PAIRS):
        cs = slice(p * LANES, (p + 1) * LANES)
        ss = _mm_exact_r(kkr[:, cs] * kkr[:, cs], bd)
        kk_o[p] = kkr[:, cs] * lax.rsqrt(jnp.maximum(ss, 1e-24))
    for dd in range(2):
        ls = slice(dd * 64, (dd + 1) * 64)
        zw = w0[dd:dd + 1, :] + _mm(tw[:, ls], w2[dd])
        lws.append(-np.float32(np.exp(-0.5)) * _sigmoid(zw))
        asig = _sigmoid(a0[dd:dd + 1, :] + _mm(ta[:, ls], a2[dd]))
        ks.append(k * (1.0 + (asig - 1.0) * k_a))
        bs.append(asig)
    rk = r * (ks[0] + ks[1]) * r_k
    g_o[...] = g
    for p in range(C_PAIRS):
        cs = slice(p * LANES, (p + 1) * LANES)
        r_o[p] = r[:, cs]
        v_o[p] = v[:, cs]
        bv_o[p] = _mm_exact_r(rk[:, cs], bd) * v[:, cs]
        lw0_o[p] = lws[0][:, cs]
        lw1_o[p] = lws[1][:, cs]
        k0_o[p] = ks[0][:, cs]
        k1_o[p] = ks[1][:, cs]
        kk = kk_o[p]
        b0_o[p] = kk * bs[0][:, cs]
        b1_o[p] = kk * bs[1][:, cs]


def _rw_in(h, mix, vecs, wr, wk, wv, w1, w2, a1, a2, g1, g2, bd, *, n_rows, t_lat, t_ctx, n_lat):
    d = h.shape[1]
    tm = 256
    assert t_ctx == tm and t_lat % tm == 0 and tm % GRID_W == 0
    nb64 = n_rows // GRID_W
    c2 = lambda i: (0, 0)
    c3 = lambda i: (0, 0, 0)
    pm_spec = pl.BlockSpec((C_PAIRS, tm, LANES), lambda i: (0, i, 0))
    pm_shape = jax.ShapeDtypeStruct((C_PAIRS, n_rows, LANES), F32)
    return pl.pallas_call(
        functools.partial(_rw_in_kernel, tm=tm, t_lat=t_lat, t_ctx=t_ctx, n_lat=n_lat),
        grid=(n_rows // tm,),
        in_specs=[pl.BlockSpec((tm, d), lambda i: (i, 0)),
                  pl.BlockSpec((GRID_W, d), lambda i: (jnp.maximum(i * (tm // GRID_W) - 1, 0), 0)),
                  pl.BlockSpec((GRID_W, d), lambda i: (jnp.minimum((i + 1) * (tm // GRID_W), nb64 - 1), 0)),
                  pl.BlockSpec((6, d), c2), pl.BlockSpec((8, d), c2),
                  pl.BlockSpec((d, d), c2), pl.BlockSpec((d, d), c2), pl.BlockSpec((d, d), c2),
                  pl.BlockSpec((d, LANES), c2), pl.BlockSpec((2, 64, d), c3),
                  pl.BlockSpec((d, LANES), c2), pl.BlockSpec((2, 64, d), c3),
                  pl.BlockSpec((d, LANES), c2), pl.BlockSpec((LANES, d), c2),
                  pl.BlockSpec((LANES, LANES), c2)],
        out_specs=[pm_spec] * 10 + [pl.BlockSpec((tm, d), lambda i: (i, 0))],
        out_shape=[pm_shape] * 10 + [jax.ShapeDtypeStruct((n_rows, d), F32)],
        compiler_params=_cparams(("parallel",)),
        name="rw_in",
    )(h, h, h, mix, vecs, wr, wk, wv, w1, w2, a1, a2, g1, g2, bd)


def _rwkv_kernel(*refs):
    ins_f, ins_b = refs[0:6], refs[6:12]
    y_f, y_b, st = refs[12:]

    @pl.when(pl.program_id(1) == 0)
    def _():
        st[...] = jnp.zeros_like(st)

    L = C_CHUNK
    ri = _iota((L, L), 0)
    ci = _iota((L, L), 1)
    tri = (jnp.where(ri >= ci, 1.0, 0.0).astype(BF16), jnp.where(ri <= ci, 1.0, 0.0).astype(BF16))
    si = _iota((2 * L, 2 * L), 0) % L
    sj = _iota((2 * L, 2 * L), 1) % L
    incl = (si >= sj, si <= sj)
    strict = (si > sj, si < sj)
    eye = jnp.where(_iota((2 * L, 2 * L), 0) == _iota((2 * L, 2 * L), 1), 1.0, 0.0)
    lo_lane = _iota((L, LANES), 1) < C_HEAD

    def stack(x):
        return jnp.concatenate([jnp.where(lo_lane, x, 0.0), jnp.where(lo_lane, 0.0, x)], axis=0)

    refs_d = (ins_f, ins_b)
    y_refs = (y_f, y_b)
    for g0 in range(0, C_PAIRS, RW_GROUP):
        chains = [(d, p) for p in range(g0, g0 + RW_GROUP) for d in range(2)]
        n = len(chains)
        cs = [_mm_exact_l(tri[d], refs_d[d][1][p]) for d, p in chains]
        big, vs, xa0r0, xbke, dec = [], [], [], [], []
        for (d, p), c in zip(chains, cs):
            r_ref, lw_ref, k_ref, v_ref, kk_ref, b_ref = refs_d[d]
            r, lw, k, v, b = r_ref[p], lw_ref[p], k_ref[p], v_ref[p], b_ref[p]
            a = -kk_ref[p]
            cex = c - lw
            if d == 0:
                mid, c_end = c[L // 2 - 1:L // 2, :], c[L - 1:L, :]
            else:
                mid, c_end = c[L // 2:L // 2 + 1, :], c[0:1, :]
            e_k = jnp.exp(mid - c)
            e_end = jnp.exp(c_end - c)
            lhs = jnp.concatenate([stack(a * jnp.exp(cex - mid)), stack(r * jnp.exp(c - mid))], axis=0)
            rhs = jnp.concatenate([stack(b * e_k), stack(k * e_k)], axis=0)
            big.append(_mm_nt(lhs, rhs))
            xa0r0.append(jnp.concatenate([stack(a * jnp.exp(cex)), stack(r * jnp.exp(c))], axis=0).astype(BF16))
            xbke.append(jnp.concatenate([stack(b * e_end), stack(k * e_end)], axis=0).astype(BF16))
            vs.append(stack(v))
            dec.append(jnp.exp(c_end))
        qs = [jnp.where(strict[d], bg[:2 * L, :2 * L], 0.0) for (d, _), bg in zip(chains, big)]
        n_ak = [jnp.where(strict[d], bg[:2 * L, 2 * L:], 0.0).astype(BF16) for (d, _), bg in zip(chains, big)]
        m_r = [jnp.where(jnp.concatenate([incl[d], incl[d]], axis=1), bg[2 * L:, :], 0.0).astype(BF16)
               for (d, _), bg in zip(chains, big)]
        ps = [eye + q for q in qs]
        for _ in range(5):
            qs = [_mm_hp(q, q) for q in qs]
            ps = [pp + _mm_hp(pp, q) for pp, q in zip(ps, qs)]
        s_prev = [st[d, p] for d, p in chains]
        sa = [_mm_nt(x, s) for x, s in zip(xa0r0, s_prev)]
        rhs = [sa[i][:2 * L] + _mm(n_ak[i], vs[i]) for i in range(n)]
        us = [_mm_hp(ps[i], rhs[i]) for i in range(n)]
        uv = [jnp.concatenate([us[i], vs[i]], axis=0).astype(BF16) for i in range(n)]
        ys = [sa[i][2 * L:] + _mm(m_r[i], uv[i]) for i in range(n)]
        for i, (d, p) in enumerate(chains):
            st[d, p] = s_prev[i] * dec[i] + _mm_tn(uv[i], xbke[i])
            y_refs[d][p] = ys[i][:L] + ys[i][L:]


def _rwkv_scan(r, v, kk, lw, k, b, *, n_rows, t_lat, t_ctx, bsz):
    ch = C_CHUNK
    nlat, nctx = t_lat // ch, t_ctx // ch
    fwd = functools.partial(_scan_block, nlat=nlat, nctx=nctx, bsz=bsz, reverse=False)
    bwd = functools.partial(_scan_block, nlat=nlat, nctx=nctx, bsz=bsz, reverse=True)
    sf = pl.BlockSpec((C_PAIRS, ch, LANES), lambda bb, s: (0, fwd(bb, s), 0))
    sb = pl.BlockSpec((C_PAIRS, ch, LANES), lambda bb, s: (0, bwd(bb, s), 0))
    return pl.pallas_call(
        _rwkv_kernel,
        grid=(bsz, nlat + nctx),
        in_specs=[sf] * 6 + [sb] * 6,
        out_specs=[sf, sb],
        out_shape=[jax.ShapeDtypeStruct((C_PAIRS, n_rows, LANES), F32)] * 2,
        scratch_shapes=[pltpu.VMEM((2, C_PAIRS, LANES, LANES), F32)],
        compiler_params=_cparams(("parallel", "arbitrary")),
        name="rwkv_scan",
    )(r, lw[0], k[0], v, kk, b[0], r, lw[1], k[1], v, kk, b[1])


def _rw_out_kernel(x_ref, m_ref, yf, yb, bv, g_ref, lnw, lnb, bd_ref, w_ref, o_ref):
    bd = bd_ref[...]
    parts = []
    for p in range(C_PAIRS):
        cs = slice(p * LANES, (p + 1) * LANES)
        y = yf[p] + yb[p]
        mu = _mm_exact_r(y, bd) * (1.0 / C_HEAD)
        yc = y - mu
        var = _mm_exact_r(yc * yc, bd) * (1.0 / C_HEAD)
        yn = yc * lax.rsqrt(var + LNX_EPS) * lnw[:, cs] + lnb[:, cs] + bv[p]
        parts.append((yn * g_ref[:, cs]).astype(BF16))
    cat = jnp.concatenate(parts, axis=-1)
    o_ref[...] = x_ref[...] + m_ref[5:6, :] * _dot(cat, w_ref[...], ((1,), (0,)))


def _rw_out(x, mod, yf, yb, bv, g, lnw, lnb, bd, w, *, n_rows, t_lat, bsz):
    d = x.shape[1]
    tm = 256
    row = lambda i: (i, 0)
    const = lambda i: (0, 0)
    pm = pl.BlockSpec((C_PAIRS, tm, LANES), lambda i: (0, i, 0))
    return pl.pallas_call(
        _rw_out_kernel,
        grid=(n_rows // tm,),
        in_specs=[pl.BlockSpec((tm, d), row),
                  pl.BlockSpec((None, N_ADA, d), lambda i: (_mod_row(i, tm, t_lat, bsz), 0, 0)),
                  pm, pm, pm, pl.BlockSpec((tm, d), row),
                  pl.BlockSpec((1, d), const), pl.BlockSpec((1, d), const),
                  pl.BlockSpec((LANES, LANES), const), pl.BlockSpec((d, d), const)],
        out_specs=pl.BlockSpec((tm, d), row),
        out_shape=jax.ShapeDtypeStruct((n_rows, d), F32),
        compiler_params=_cparams(("parallel",)),
        name="rw_out",
    )(x, mod, yf, yb, bv, g, lnw.reshape(1, d), lnb.reshape(1, d), bd, w)


def _even_layer(x, mod, g, wi, wo, in_w, out_w, lb, hgrn_g, conv_w, conv_b, dt_bias, a_log, d_skip, ssd_g, dims):
    n_all, t_lat, t_ctx, bsz = dims["n_all"], dims["t_lat"], dims["t_ctx"], dims["bsz"]
    n_lat = bsz * t_lat
    x, h = _ffn_half(x, mod, g[0], wi[0], wo[0], slot=0, n_rows=n_all, t_lat=t_lat, bsz=bsz,
                     post="prenorm", g2=g[1], h_dtype=BF16)
    w_pad = jnp.pad(in_w, ((0, 0), (0, EVEN_IN_PAD - EVEN_IN))).astype(BF16)
    dtb = jnp.pad(dt_bias.reshape(1, 2 * B_HEADS), ((0, 0), (0, LANES - 2 * B_HEADS)))
    qi, f, gz, xbc, dt = _even_in(h, w_pad, lb.reshape(1, 2 * A_DIM), dtb, n_all)
    xbc = _conv_silu(xbc, conv_w, conv_b, n_rows=n_all, t_lat=t_lat, t_ctx=t_ctx, n_lat=n_lat)
    oaf, oab = _hgrn_scan(qi, f, n_rows=n_all, t_lat=t_lat, t_ctx=t_ctx, bsz=bsz)
    a_row = jnp.pad(-jnp.exp(a_log.astype(F32)).reshape(1, 2 * B_HEADS), ((0, 0), (0, LANES - 2 * B_HEADS)))
    dsk_row = jnp.repeat(d_skip.astype(F32), B_HEADDIM).reshape(1, B_DIM)
    e_np = np.zeros((2, LANES, B_DIM), np.float32)
    for dd in range(2):
        for hh in range(B_HEADS):
            e_np[dd, dd * B_HEADS + hh, hh * B_HEADDIM:(hh + 1) * B_HEADDIM] = 1.0
    ybf, ybb = _ssd_scan(xbc, dt, a_row, dsk_row, jnp.asarray(e_np, BF16),
                         n_rows=n_all, t_lat=t_lat, t_ctx=t_ctx, bsz=bsz)
    x = _even_out(x, mod, oaf, oab, ybf, ybb, gz, hgrn_g.reshape(1, A_DK), ssd_g.reshape(1, B_DIM),
                  out_w.astype(BF16), n_rows=n_all, t_lat=t_lat, bsz=bsz)
    return x


def _odd_layer(x, mod, g, wi, wo, p, dims):
    n_all, t_lat, t_ctx, bsz = dims["n_all"], dims["t_lat"], dims["t_ctx"], dims["bsz"]
    n_lat = bsz * t_lat
    d = D_MODEL
    x, h = _ffn_half(x, mod, g[0], wi[0], wo[0], slot=0, n_rows=n_all, t_lat=t_lat, bsz=bsz,
                     post="prenorm", g2=g[1], h_dtype=F32)
    vecs = jnp.concatenate([p["w0"], p["a0"], p["k_k"][None], p["k_a"][None], p["r_k"].reshape(1, d),
                            jnp.zeros((1, d), F32)], axis=0)
    w1 = jnp.concatenate([p["w1"][0], p["w1"][1]], axis=1).astype(BF16)
    a1 = jnp.concatenate([p["a1"][0], p["a1"][1]], axis=1).astype(BF16)
    bd_np = np.kron(np.eye(2, dtype=np.float32), np.ones((C_HEAD, C_HEAD), np.float32))
    bd = jnp.asarray(bd_np, BF16)
    outs = _rw_in(h, p["x_mix"], vecs, p["wr"].astype(BF16), p["wk"].astype(BF16), p["wv"].astype(BF16),
                  w1, p["w2"].astype(BF16), a1, p["a2"].astype(BF16), p["g1"].astype(BF16),
                  p["g2"].astype(BF16), bd, n_rows=n_all, t_lat=t_lat, t_ctx=t_ctx, n_lat=n_lat)
    r, v, kk, bv, lw0, lw1, k0, k1, b0, b1, gg = outs
    yf, yb = _rwkv_scan(r, v, kk, (lw0, lw1), (k0, k1), (b0, b1), n_rows=n_all, t_lat=t_lat, t_ctx=t_ctx, bsz=bsz)
    x = _rw_out(x, mod, yf, yb, bv, gg, p["ln_w"], p["ln_b"], bd, p["wo"].astype(BF16),
                n_rows=n_all, t_lat=t_lat, bsz=bsz)
    return x


def kernel(x, c, ctx, c_ctx, ada_w, ada_b, norm_g, ffn_wi, ffn_wo, final_g, hgrn_lb_logits, ev_in_w, ev_out_w, hgrn_norm_g, ssd_conv_w, ssd_conv_b, ssd_dt_bias, ssd_a_log, ssd_d, ssd_norm_g, rw_x_mix, rw_wr, rw_wk, rw_wv, rw_wo, rw_w0, rw_w1, rw_w2, rw_a0, rw_a1, rw_a2, rw_g1, rw_g2, rw_k_k, rw_k_a, rw_r_k, rw_ln_w, rw_ln_b):
    bsz, t_lat, d = x.shape
    t_ctx = ctx.shape[1]
    depth = ada_w.shape[0]
    n_lat = bsz * t_lat
    n_all = n_lat + bsz * t_ctx
    dims = dict(n_all=n_all, t_lat=t_lat, t_ctx=t_ctx, bsz=bsz)
    assert bsz + 1 <= 8

    xs = jnp.concatenate([x.reshape(n_lat, d), ctx.reshape(bsz * t_ctx, d)], axis=0)
    cvec = jnp.concatenate([c, c_ctx[None], jnp.zeros((8 - bsz - 1, d), F32)], axis=0)
    mod_all = _modulation(cvec, ada_w, ada_b).reshape(depth, 8, N_ADA, d)
    lb_all = jnp.cumsum(jax.nn.softmax(hgrn_lb_logits.astype(F32), axis=0), axis=0)
    wi_bf = ffn_wi.astype(BF16)
    wo_bf = ffn_wo.astype(BF16)

    for layer in range(depth):
        last = layer == depth - 1
        mod = mod_all[layer]
        g = norm_g[layer]
        if layer % 2 == 0:
            e = layer // 2
            xs = _even_layer(xs, mod, g, wi_bf[layer], wo_bf[layer], ev_in_w[e], ev_out_w[e], lb_all[layer],
                             hgrn_norm_g[e], ssd_conv_w[e], ssd_conv_b[e], ssd_dt_bias[e], ssd_a_log[e],
                             ssd_d[e], ssd_norm_g[e], dims)
        else:
            o = layer // 2
            p = dict(x_mix=rw_x_mix[o], wr=rw_wr[o], wk=rw_wk[o], wv=rw_wv[o], wo=rw_wo[o], w0=rw_w0[o],
                     w1=rw_w1[o], w2=rw_w2[o], a0=rw_a0[o], a1=rw_a1[o], a2=rw_a2[o], g1=rw_g1[o], g2=rw_g2[o],
                     k_k=rw_k_k[o], k_a=rw_k_a[o], r_k=rw_r_k[o], ln_w=rw_ln_w[o], ln_b=rw_ln_b[o])
            xs = _odd_layer(xs, mod, g, wi_bf[layer], wo_bf[layer], p, dims)
        if last:
            out = _ffn_half(xs, mod, g[2], wi_bf[layer, 1], wo_bf[layer, 1], slot=2, n_rows=n_lat, t_lat=t_lat,
                            bsz=bsz, post="final", g2=final_g)
        else:
            xs = _ffn_half(xs, mod, g[2], wi_bf[layer, 1], wo_bf[layer, 1], slot=2, n_rows=n_all, t_lat=t_lat,
                           bsz=bsz)
    return out.reshape(bsz, t_lat, d)
```

```python
import functools

import numpy as np
import jax
import jax.numpy as jnp
from jax import lax
from jax.experimental import pallas as pl
from jax.experimental.pallas import tpu as pltpu

F32 = jnp.float32
BF16 = jnp.bfloat16

D_MODEL = 1024
N_ADA = 9
FF_DIM = 2816
EPS = 1e-6
GRID_W = 64
A_HEADS = 4
A_DK = 128
A_DIM = A_HEADS * A_DK
A_CHUNK = 32
B_HEADS = 8
B_HEADDIM = 64
B_DIM = B_HEADS * B_HEADDIM
B_GROUPS = 2
B_STATE = 128
B_CHUNK = 128
B_XBC = B_DIM + 2 * B_GROUPS * B_STATE
EVEN_IN = A_DIM * 5 + B_DIM + B_XBC + 2 * B_HEADS
EVEN_IN_PAD = 4224
C_HEAD = 64
C_HEADS = D_MODEL // C_HEAD
C_PAIRS = C_HEADS // 2
C_CHUNK = 64
LNX_EPS = 64e-5
RW_GROUP = 8
LANES = 128

VMEM_LIMIT = 56 * 1024 * 1024


def _cparams(sem):
    return pltpu.CompilerParams(dimension_semantics=sem, vmem_limit_bytes=VMEM_LIMIT)


def _sigmoid(x):
    return 1.0 / (1.0 + jnp.exp(-x))


def _silu(x):
    return x * _sigmoid(x)


def _softplus(x):
    return jnp.maximum(x, 0.0) + jnp.log1p(jnp.exp(-jnp.abs(x)))


def _dot(a, b, dims):
    return lax.dot_general(a, b, (dims, ((), ())), preferred_element_type=F32)


def _mm(a, b):
    return _dot(a.astype(BF16), b.astype(BF16), ((1,), (0,)))


def _mm_nt(a, b):
    return _dot(a.astype(BF16), b.astype(BF16), ((1,), (1,)))


def _mm_tn(a, b):
    return _dot(a.astype(BF16), b.astype(BF16), ((0,), (0,)))


def _split2(x):
    hi = x.astype(BF16)
    lo = (x - hi.astype(F32)).astype(BF16)
    return hi, lo


def _split3(x):
    x1 = x.astype(BF16)
    r1 = x - x1.astype(F32)
    x2 = r1.astype(BF16)
    x3 = (r1 - x2.astype(F32)).astype(BF16)
    return x1, x2, x3


def _cumsum_rows(x, reverse):
    n = x.shape[0]
    row = _iota(x.shape, 0)
    s = 1
    while s < n:
        if reverse:
            x = x + jnp.where(row < n - s, pltpu.roll(x, n - s, 0), 0.0)
        else:
            x = x + jnp.where(row >= s, pltpu.roll(x, s, 0), 0.0)
        s *= 2
    return x


def _mm_exact_r(x, m_bf):
    x1, x2, x3 = _split3(x)
    dn = ((1,), (0,))
    return _dot(x1, m_bf, dn) + _dot(x2, m_bf, dn) + _dot(x3, m_bf, dn)


def _mm_hp(a, b):
    ah, al = _split2(a)
    bh, bl = _split2(b)
    dn = ((1,), (0,))
    return _dot(ah, bh, dn) + _dot(ah, bl, dn) + _dot(al, bh, dn)


def _rms(x):
    return x * lax.rsqrt(jnp.mean(x * x, axis=-1, keepdims=True) + EPS)


def _iota(shape, dim):
    return lax.broadcasted_iota(jnp.int32, shape, dim)


def _mod_row(i, tm, t_lat, bsz):
    return jnp.minimum((i * tm) // t_lat, bsz)


def _seq_pos(r0, t_lat, t_ctx, n_lat):
    in_lat = r0 < n_lat
    pos = jnp.where(in_lat, r0 % t_lat, (r0 - n_lat) % t_ctx)
    slen = jnp.where(in_lat, t_lat, t_ctx)
    return in_lat, pos, slen


def _scan_block(b, s, nlat, nctx, bsz, reverse):
    if reverse:
        cblk = bsz * nlat + b * nctx + (nctx - 1 - s)
        lblk = b * nlat + (nlat - 1 - (s - nctx))
    else:
        cblk = bsz * nlat + b * nctx + s
        lblk = b * nlat + (s - nctx)
    return jnp.where(s < nctx, cblk, lblk)


def _mod_kernel(c_ref, w_ref, b_ref, o_ref):
    sc = _silu(c_ref[...])
    o_ref[...] = _mm(sc, w_ref[...]) + b_ref[...]


def _modulation(cvec, ada_w, ada_b):
    depth, d, n = ada_w.shape
    tn = 1536
    return pl.pallas_call(
        _mod_kernel,
        grid=(depth, n // tn),
        in_specs=[pl.BlockSpec((8, d), lambda l, j: (0, 0)),
                  pl.BlockSpec((None, d, tn), lambda l, j: (l, 0, j)),
                  pl.BlockSpec((None, 1, tn), lambda l, j: (l, 0, j))],
        out_specs=pl.BlockSpec((None, 8, tn), lambda l, j: (l, 0, j)),
        out_shape=jax.ShapeDtypeStruct((depth, 8, n), F32),
        compiler_params=_cparams(("arbitrary", "arbitrary")),
        name="ada_mod",
    )(cvec, ada_w, ada_b.reshape(depth, 1, n))


def _ffn_kernel(*refs, slot, tf, post, n_lat_tiles):
    if n_lat_tiles is None:
        x_ref = refs[0]
        x = x_ref[...]
    else:
        x = jnp.where(pl.program_id(0) >= n_lat_tiles, refs[1][...], refs[0][...])
        refs = refs[1:]
    m_ref, g_ref, wi_ref, wo_ref = refs[1:5]
    if post == "prenorm":
        g2_ref, o_ref, h_ref, acc = refs[5:]
    elif post == "final":
        g2_ref, o_ref, acc = refs[5:]
    else:
        o_ref, acc = refs[5:]
    dn = ((1,), (0,))
    nch = FF_DIM // tf
    h = _rms(x) * g_ref[...] * (1.0 + m_ref[3 * slot + 1:3 * slot + 2, :]) + m_ref[3 * slot:3 * slot + 1, :]
    h = h.astype(BF16)

    def gate_up(c):
        return (_dot(h, wi_ref[:, c * tf:(c + 1) * tf], dn),
                _dot(h, wi_ref[:, FF_DIM + c * tf:FF_DIM + (c + 1) * tf], dn))

    nxt = gate_up(0)
    for c in range(nch):
        gt, up = nxt
        if c + 1 < nch:
            nxt = gate_up(c + 1)
        part = _dot((_silu(gt) * up).astype(BF16), wo_ref[c * tf:(c + 1) * tf, :], dn)
        if c == 0:
            acc[...] = part
        else:
            acc[...] += part

    xn = x + 0.5 * m_ref[3 * slot + 2:3 * slot + 3, :] * acc[...]
    if post == "final":
        o_ref[...] = _rms(xn) * g2_ref[...]
    else:
        o_ref[...] = xn
    if post == "prenorm":
        h2 = _rms(xn) * g2_ref[...] * (1.0 + m_ref[4:5, :]) + m_ref[3:4, :]
        h_ref[...] = h2.astype(h_ref.dtype)


def _ffn_half(x, mod, g, wi_all, wo_all, sel, *, slot, n_rows, t_lat, bsz, post=None, g2=None, h_dtype=F32):
    split = isinstance(x, tuple)
    d = D_MODEL
    tm = 512
    while n_rows % tm or t_lat % tm:
        tm //= 2
    const = lambda i: (0, 0)
    resident = pl.Buffered(1)
    if split:
        n_lat_tiles = x[0].shape[0] // tm
        x_specs = [pl.BlockSpec((tm, d), lambda i: (jnp.minimum(i, n_lat_tiles - 1), 0)),
                   pl.BlockSpec((tm, d), lambda i: (jnp.maximum(i - n_lat_tiles, 0), 0))]
        x_args = list(x)
    else:
        n_lat_tiles = None
        x_specs = [pl.BlockSpec((tm, d), lambda i: (i, 0))]
        x_args = [x]
    in_specs = x_specs + [
        pl.BlockSpec((None, N_ADA, d), lambda i: (_mod_row(i, tm, t_lat, bsz), 0, 0)),
        pl.BlockSpec((1, d), const),
        pl.BlockSpec((None, None, d, 2 * FF_DIM), lambda i: sel + (0, 0), pipeline_mode=resident),
        pl.BlockSpec((None, None, FF_DIM, d), lambda i: sel + (0, 0), pipeline_mode=resident)]
    args = x_args + [mod, g.reshape(1, d), wi_all, wo_all]
    out_spec = pl.BlockSpec((tm, d), lambda i: (i, 0))
    out_shape = jax.ShapeDtypeStruct((n_rows, d), F32)
    if post is not None:
        in_specs.append(pl.BlockSpec((1, d), const))
        args.append(g2.reshape(1, d))
    if post == "prenorm":
        out_specs = [out_spec, out_spec]
        out_shapes = [out_shape, jax.ShapeDtypeStruct((n_rows, d), h_dtype)]
    else:
        out_specs, out_shapes = out_spec, out_shape
    return pl.pallas_call(
        functools.partial(_ffn_kernel, slot=slot, tf=256, post=post, n_lat_tiles=n_lat_tiles),
        grid=(n_rows // tm,),
        in_specs=in_specs,
        out_specs=out_specs,
        out_shape=out_shapes,
        scratch_shapes=[pltpu.VMEM((tm, d), F32)],
        compiler_params=_cparams(("parallel",)),
        name="ffn_half",
    )(*args)


def _even_in_kernel(h_ref, w_ref, lb_ref, dtb_ref, qi_ref, f_ref, gz_ref, xbc_ref, dt_ref):
    h = h_ref[...]
    dn = ((1,), (0,))
    qi = _dot(h, w_ref[:, 0:1024], dn)
    qi_ref[:, 0:512] = _silu(qi[:, 0:512])
    qi_ref[:, 512:1024] = qi[:, 512:1024]
    lb = lb_ref[...]
    f_ref[...] = lb + (1.0 - lb) * _sigmoid(_dot(h, w_ref[:, 1024:2048], dn))
    gz_ref[...] = _silu(_dot(h, w_ref[:, 2048:3072], dn))
    xbc_ref[...] = _dot(h, w_ref[:, 3072:4096], dn)
    dt_ref[...] = _softplus(_dot(h, w_ref[:, 4096:4224], dn) + dtb_ref[...])


def _even_in(h, w, lb, dtb, n_rows):
    d = h.shape[1]
    tm = 512
    while n_rows % tm:
        tm //= 2
    widths = (1024, 1024, 1024, 1024, LANES)
    return pl.pallas_call(
        _even_in_kernel,
        grid=(n_rows // tm,),
        in_specs=[pl.BlockSpec((tm, d), lambda i: (i, 0)),
                  pl.BlockSpec((d, EVEN_IN_PAD), lambda i: (0, 0)),
                  pl.BlockSpec((1, 1024), lambda i: (0, 0)),
                  pl.BlockSpec((1, LANES), lambda i: (0, 0))],
        out_specs=[pl.BlockSpec((tm, wd), lambda i: (i, 0)) for wd in widths],
        out_shape=[jax.ShapeDtypeStruct((n_rows, wd), F32) for wd in widths],
        compiler_params=_cparams(("parallel",)),
        name="even_in",
    )(h, w, lb, dtb)


def _conv_kernel(x_ref, p_ref, n_ref, w_ref, b_ref, o_ref, *, tm, t_lat, t_ctx, n_lat):
    r0 = pl.program_id(0) * tm
    _, pos, slen = _seq_pos(r0, t_lat, t_ctx, n_lat)
    first = pos == 0
    last = pos + tm == slen
    x = x_ref[...]
    row = _iota((tm, 1), 0)
    prev_row = jnp.where(first, 0.0, p_ref[7:8, :])
    next_row = jnp.where(last, 0.0, n_ref[0:1, :])
    xp = jnp.where(row == 0, prev_row, pltpu.roll(x, 1, 0))
    xn = jnp.where(row == tm - 1, next_row, pltpu.roll(x, tm - 1, 0))
    y = w_ref[0:1, :] * xp + w_ref[1:2, :] * x + w_ref[2:3, :] * xn + b_ref[...]
    o_ref[...] = _silu(y)


def _conv_silu(xbc, w, b, *, n_rows, t_lat, t_ctx, n_lat):
    c = xbc.shape[1]
    tm = 256
    nb8 = n_rows // 8
    return pl.pallas_call(
        functools.partial(_conv_kernel, tm=tm, t_lat=t_lat, t_ctx=t_ctx, n_lat=n_lat),
        grid=(n_rows // tm,),
        in_specs=[pl.BlockSpec((tm, c), lambda i: (i, 0)),
                  pl.BlockSpec((8, c), lambda i: (jnp.maximum(i * (tm // 8) - 1, 0), 0)),
                  pl.BlockSpec((8, c), lambda i: (jnp.minimum((i + 1) * (tm // 8), nb8 - 1), 0)),
                  pl.BlockSpec((3, c), lambda i: (0, 0)),
                  pl.BlockSpec((1, c), lambda i: (0, 0))],
        out_specs=pl.BlockSpec((tm, c), lambda i: (i, 0)),
        out_shape=jax.ShapeDtypeStruct((n_rows, c), F32),
        compiler_params=_cparams(("parallel",)),
        name="ssd_conv",
    )(xbc, xbc, xbc, w, b.reshape(1, c))


def _hgrn_kernel(qi_f, f_f, qi_b, f_b, o_f, o_b, st, *, ch):
    @pl.when(pl.program_id(1) == 0)
    def _():
        st[...] = jnp.zeros_like(st)

    L = A_CHUNK
    nsub = ch // L
    ri = _iota((L, L), 0)
    ci = _iota((L, L), 1)
    incl = (ri >= ci, ri <= ci)
    qis, fs, outs = (qi_f, qi_b), (f_f, f_b), (o_f, o_b)

    chains = [(d, h) for h in range(A_HEADS) for d in range(2)]

    def body(j, carry):
        r0s = (pl.multiple_of(j * L, L), pl.multiple_of((nsub - 1 - j) * L, L))
        qs, vs, ks, logfs = [], [], [], []
        for d, h in chains:
            cs = slice(h * A_DK, (h + 1) * A_DK)
            rows = pl.ds(r0s[d], L)
            qs.append(qis[d][rows, cs])
            vs.append(qis[d][rows, A_DIM + h * A_DK:A_DIM + (h + 1) * A_DK].astype(BF16))
            f = fs[d][rows, cs]
            ks.append(1.0 - f)
            logfs.append(jnp.log(f))
        bcs = [_cumsum_rows(lf, d == 1) for (d, _), lf in zip(chains, logfs)]
        s_prev = [st[d, h] for d, h in chains]
        dss, scs, o2s, decs = [], [], [], []
        for i, (d, h) in enumerate(chains):
            bc = bcs[i]
            if d == 0:
                blast, bref = bc[L - 1:L, :], bc[L // 2 - 1:L // 2, :]
            else:
                blast, bref = bc[0:1, :], bc[L // 2:L // 2 + 1, :]
            dss.append(_mm_tn(vs[i], ks[i] * jnp.exp(blast - bc)))
            scs.append(_mm_nt(qs[i] * jnp.exp(bc - bref), ks[i] * jnp.exp(bref - bc)))
            o2s.append(_mm_nt(qs[i] * jnp.exp(bc), s_prev[i]))
            decs.append(jnp.exp(blast))
        o1s = [_mm(jnp.where(incl[d], sc, 0.0), v) for (d, _), sc, v in zip(chains, scs, vs)]
        for i, (d, h) in enumerate(chains):
            st[d, h] = decs[i] * s_prev[i] + dss[i]
            outs[d][pl.ds(r0s[d], L), h * A_DK:(h + 1) * A_DK] = o1s[i] + o2s[i]
        return carry

    lax.fori_loop(0, nsub, body, 0)


def _hgrn_scan(qi, f, *, n_rows, t_lat, t_ctx, bsz):
    ch = 256
    nlat, nctx = t_lat // ch, t_ctx // ch
    fwd = functools.partial(_scan_block, nlat=nlat, nctx=nctx, bsz=bsz, reverse=False)
    bwd = functools.partial(_scan_block, nlat=nlat, nctx=nctx, bsz=bsz, reverse=True)
    return pl.pallas_call(
        functools.partial(_hgrn_kernel, ch=ch),
        grid=(bsz, nlat + nctx),
        in_specs=[pl.BlockSpec((ch, 2 * A_DIM), lambda b, s: (fwd(b, s), 0)),
                  pl.BlockSpec((ch, A_DIM), lambda b, s: (fwd(b, s), 0)),
                  pl.BlockSpec((ch, 2 * A_DIM), lambda b, s: (bwd(b, s), 0)),
                  pl.BlockSpec((ch, A_DIM), lambda b, s: (bwd(b, s), 1))],
        out_specs=[pl.BlockSpec((ch, A_DIM), lambda b, s: (fwd(b, s), 0)),
                   pl.BlockSpec((ch, A_DIM), lambda b, s: (bwd(b, s), 0))],
        out_shape=[jax.ShapeDtypeStruct((n_rows, A_DIM), F32)] * 2,
        scratch_shapes=[pltpu.VMEM((2, A_HEADS, A_DK, A_DK), F32)],
        compiler_params=_cparams(("parallel", "arbitrary")),
        name="hgrn_scan",
    )(qi, f, qi, f)


def _ssd_kernel(x_f, dt_f, x_b, dt_b, a_ref, dsk_ref, e_ref, o_f, o_b, st):
    @pl.when(pl.program_id(1) == 0)
    def _():
        st[...] = jnp.zeros_like(st)

    L = B_CHUNK
    hg = B_HEADS // B_GROUPS
    gw = hg * B_HEADDIM
    ri = _iota((L, L), 0)
    ci = _iota((L, L), 1)
    incl = (ri >= ci, ri <= ci)
    xs, dts, outs = (x_f, x_b), (dt_f, dt_b), (o_f, o_b)
    dt = [dts[d][...] for d in range(2)]
    acum = [_cumsum_rows(dt[d] * a_ref[...], d == 1) for d in range(2)]
    acum_e = [_mm_exact_r(acum[d], e_ref[d]) for d in range(2)]
    dt_e = [_mm_exact_r(dt[d], e_ref[d]) for d in range(2)]
    alast_e = [acum_e[0][L - 1:L, :], acum_e[1][0:1, :]]
    acum_t = [acum[d].T for d in range(2)]
    xa = [xs[d][:, 0:B_DIM] for d in range(2)]
    xdt = [xa[d] * dt_e[d] for d in range(2)]
    xdec = [xdt[d] * jnp.exp(alast_e[d] - acum_e[d]) for d in range(2)]
    dg = [(d, g) for d in range(2) for g in range(B_GROUPS)]
    bgs = [xs[d][:, B_DIM + g * B_STATE:B_DIM + (g + 1) * B_STATE].astype(BF16) for d, g in dg]
    cgs = [xs[d][:, B_DIM + (B_GROUPS + g) * B_STATE:B_DIM + (B_GROUPS + g + 1) * B_STATE].astype(BF16)
           for d, g in dg]
    cbs = [_mm_nt(cgs[i], bgs[i]) for i in range(len(dg))]
    s_prev = [st[d, g] for d, g in dg]
    y_off = [_mm(cgs[i], s_prev[i]) * jnp.exp(acum_e[d][:, g * gw:(g + 1) * gw]) for i, (d, g) in enumerate(dg)]
    for i, (d, g) in enumerate(dg):
        gs = slice(g * gw, (g + 1) * gw)
        st[d, g] = s_prev[i] * jnp.exp(alast_e[d][:, gs]) + _mm_tn(bgs[i], xdec[d][:, gs])
    ys = []
    for i, (d, g) in enumerate(dg):
        for hh in range(hg):
            h = g * hg + hh
            j = d * B_HEADS + h
            seg = acum[d][:, j:j + 1] - acum_t[d][j:j + 1, :]
            lm = jnp.exp(jnp.where(incl[d], seg, -1e30))
            ys.append(_mm(cbs[i] * lm, xdt[d][:, h * B_HEADDIM:(h + 1) * B_HEADDIM]))
    for i, (d, g) in enumerate(dg):
        for hh in range(hg):
            h = g * hg + hh
            hs = slice(h * B_HEADDIM, (h + 1) * B_HEADDIM)
            y = ys[i * hg + hh] + y_off[i][:, hh * B_HEADDIM:(hh + 1) * B_HEADDIM]
            if d == 0:
                y = y + dsk_ref[:, hs] * xa[d][:, hs]
            outs[d][:, hs] = y


def _ssd_scan(xbc, dt, a_row, dsk_row, e_mat, *, n_rows, t_lat, t_ctx, bsz):
    ch = B_CHUNK
    nlat, nctx = t_lat // ch, t_ctx // ch
    fwd = functools.partial(_scan_block, nlat=nlat, nctx=nctx, bsz=bsz, reverse=False)
    bwd = functools.partial(_scan_block, nlat=nlat, nctx=nctx, bsz=bsz, reverse=True)
    const2 = lambda b, s: (0, 0)
    return pl.pallas_call(
        _ssd_kernel,
        grid=(bsz, nlat + nctx),
        in_specs=[pl.BlockSpec((ch, B_XBC), lambda b, s: (fwd(b, s), 0)),
                  pl.BlockSpec((ch, LANES), lambda b, s: (fwd(b, s), 0)),
                  pl.BlockSpec((ch, B_XBC), lambda b, s: (bwd(b, s), 0)),
                  pl.BlockSpec((ch, LANES), lambda b, s: (bwd(b, s), 0)),
                  pl.BlockSpec((1, LANES), const2),
                  pl.BlockSpec((1, B_DIM), const2),
                  pl.BlockSpec((2, LANES, B_DIM), lambda b, s: (0, 0, 0))],
        out_specs=[pl.BlockSpec((ch, B_DIM), lambda b, s: (fwd(b, s), 0)),
                   pl.BlockSpec((ch, B_DIM), lambda b, s: (bwd(b, s), 0))],
        out_shape=[jax.ShapeDtypeStruct((n_rows, B_DIM), F32)] * 2,
        scratch_shapes=[pltpu.VMEM((2, B_GROUPS, B_STATE, (B_HEADS // B_GROUPS) * B_HEADDIM), F32)],
        compiler_params=_cparams(("parallel", "arbitrary")),
        name="ssd_scan",
    )(xbc, dt, xbc, dt, a_row, dsk_row, e_mat)


def _even_out_kernel(x_ref, m_ref, oaf, oab, ybf, ybb, gz_ref, ga_ref, gb_ref, w_ref, o_ref):
    oa = oaf[...] + oab[...]
    parts = []
    for h in range(A_HEADS):
        cs = slice(h * A_DK, (h + 1) * A_DK)
        parts.append(_rms(oa[:, cs]) * ga_ref[...] * gz_ref[:, cs])
    ob = (ybf[...] + ybb[...]) * gz_ref[:, A_DIM:A_DIM + B_DIM]
    gw = B_DIM // B_GROUPS
    for g in range(B_GROUPS):
        cs = slice(g * gw, (g + 1) * gw)
        parts.append(_rms(ob[:, cs]) * gb_ref[:, cs])
    cat = jnp.concatenate(parts, axis=-1).astype(BF16)
    y = _dot(cat, w_ref[...], ((1,), (0,)))
    o_ref[...] = x_ref[...] + m_ref[5:6, :] * y


def _even_out(x, mod, oaf, oab, ybf, ybb, gz, ga, gb, w, *, n_rows, t_lat, bsz):
    d = x.shape[1]
    tm = 512
    while n_rows % tm or t_lat % tm:
        tm //= 2
    row = lambda i: (i, 0)
    const = lambda i: (0, 0)
    return pl.pallas_call(
        _even_out_kernel,
        grid=(n_rows // tm,),
        in_specs=[pl.BlockSpec((tm, d), row),
                  pl.BlockSpec((None, N_ADA, d), lambda i: (_mod_row(i, tm, t_lat, bsz), 0, 0)),
                  pl.BlockSpec((tm, A_DIM), row), pl.BlockSpec((tm, A_DIM), row),
                  pl.BlockSpec((tm, B_DIM), row), pl.BlockSpec((tm, B_DIM), row),
                  pl.BlockSpec((tm, A_DIM + B_DIM), row),
                  pl.BlockSpec((1, A_DK), const), pl.BlockSpec((1, B_DIM), const),
                  pl.BlockSpec((A_DIM + B_DIM, d), const)],
        out_specs=pl.BlockSpec((tm, d), row),
        out_shape=jax.ShapeDtypeStruct((n_rows, d), F32),
        compiler_params=_cparams(("parallel",)),
        name="even_out",
    )(x, mod, oaf, oab, ybf, ybb, gz, ga, gb, w)


def _rw_in_kernel(h_ref, up_ref, dn_ref, mix_ref, vec_ref, wr, wk, wv, w1, w2, a1, a2, g1, g2, bd_ref,
                  r_o, v_o, kk_o, bv_o, lw0_o, lw1_o, k0_o, k1_o, b0_o, b1_o, g_o,
                  *, tm, t_lat, t_ctx, n_lat):
    d = D_MODEL
    r0 = pl.program_id(0) * tm
    in_lat, pos, slen = _seq_pos(r0, t_lat, t_ctx, n_lat)
    first = pos == 0
    last = pos + tm == slen
    row = _iota((tm, 1), 0)
    col = row % GRID_W
    ok_prev = jnp.where(in_lat, col, row) != 0
    ok_next = jnp.where(in_lat, col - (GRID_W - 1), row - (tm - 1)) != 0
    q4 = d // 4

    def prev_of(x):
        return jnp.where(ok_prev, pltpu.roll(x, 1, 0), 0.0)

    def next_of(x):
        return jnp.where(ok_next, pltpu.roll(x, tm - 1, 0), 0.0)

    h = h_ref[...]
    h2, h3 = h[:, 2 * q4:3 * q4], h[:, 3 * q4:]
    up = jnp.concatenate([jnp.where(first, 0.0, up_ref[:, 2 * q4:3 * q4]), h2[:tm - GRID_W]], axis=0)
    down = jnp.concatenate([h3[GRID_W:], jnp.where(last, 0.0, dn_ref[:, 3 * q4:])], axis=0)
    hs = jnp.concatenate([prev_of(h[:, :q4]), next_of(h[:, q4:2 * q4]),
                          jnp.where(in_lat, up, prev_of(h2)),
                          jnp.where(in_lat, down, next_of(h3))], axis=1)
    xx = hs - h
    xr, xw, xk, xv, xa, xg = [(h + xx * mix_ref[j:j + 1, :]).astype(BF16) for j in range(6)]
    dn = ((1,), (0,))
    r = _dot(xr, wr[...], dn)
    k = _dot(xk, wk[...], dn)
    v = _dot(xv, wv[...], dn)
    g = _mm(_sigmoid(_dot(xg, g1[...], dn)), g2[...])
    tw = jnp.tanh(_dot(xw, w1[...], dn))
    ta = _dot(xa, a1[...], dn)
    w0, a0 = vec_ref[0:2, :], vec_ref[2:4, :]
    k_k, k_a, r_k = vec_ref[4:5, :], vec_ref[5:6, :], vec_ref[6:7, :]
    bd = bd_ref[...]

    lws, ks, bs = [], [], []
    kkr = k * k_k
    for p in range(C_PAIRS):
        cs = slice(p * LANES, (p + 1) * LANES)
        ss = _mm_exact_r(kkr[:, cs] * kkr[:, cs], bd)
        kk_o[p] = kkr[:, cs] * lax.rsqrt(jnp.maximum(ss, 1e-24))
    for dd in range(2):
        ls = slice(dd * 64, (dd + 1) * 64)
        zw = w0[dd:dd + 1, :] + _mm(tw[:, ls], w2[dd])
        lws.append(-np.float32(np.exp(-0.5)) * _sigmoid(zw))
        asig = _sigmoid(a0[dd:dd + 1, :] + _mm(ta[:, ls], a2[dd]))
        ks.append(k * (1.0 + (asig - 1.0) * k_a))
        bs.append(asig)
    rk = r * (ks[0] + ks[1]) * r_k
    g_o[...] = g
    for p in range(C_PAIRS):
        cs = slice(p * LANES, (p + 1) * LANES)
        r_o[p] = r[:, cs]
        v_o[p] = v[:, cs]
        bv_o[p] = _mm_exact_r(rk[:, cs], bd) * v[:, cs]
        lw0_o[p] = lws[0][:, cs]
        lw1_o[p] = lws[1][:, cs]
        k0_o[p] = ks[0][:, cs]
        k1_o[p] = ks[1][:, cs]
        kk = kk_o[p]
        b0_o[p] = kk * bs[0][:, cs]
        b1_o[p] = kk * bs[1][:, cs]


def _rw_in(h, mix, vecs, wr, wk, wv, w1, w2, a1, a2, g1, g2, bd, *, n_rows, t_lat, t_ctx, n_lat):
    d = h.shape[1]
    tm = 256
    assert t_ctx == tm and t_lat % tm == 0 and tm % GRID_W == 0
    nb64 = n_rows // GRID_W
    c2 = lambda i: (0, 0)
    c3 = lambda i: (0, 0, 0)
    pm_spec = pl.BlockSpec((C_PAIRS, tm, LANES), lambda i: (0, i, 0))
    pm_shape = jax.ShapeDtypeStruct((C_PAIRS, n_rows, LANES), F32)
    return pl.pallas_call(
        functools.partial(_rw_in_kernel, tm=tm, t_lat=t_lat, t_ctx=t_ctx, n_lat=n_lat),
        grid=(n_rows // tm,),
        in_specs=[pl.BlockSpec((tm, d), lambda i: (i, 0)),
                  pl.BlockSpec((GRID_W, d), lambda i: (jnp.maximum(i * (tm // GRID_W) - 1, 0), 0)),
                  pl.BlockSpec((GRID_W, d), lambda i: (jnp.minimum((i + 1) * (tm // GRID_W), nb64 - 1), 0)),
                  pl.BlockSpec((6, d), c2), pl.BlockSpec((8, d), c2),
                  pl.BlockSpec((d, d), c2), pl.BlockSpec((d, d), c2), pl.BlockSpec((d, d), c2),
                  pl.BlockSpec((d, LANES), c2), pl.BlockSpec((2, 64, d), c3),
                  pl.BlockSpec((d, LANES), c2), pl.BlockSpec((2, 64, d), c3),
                  pl.BlockSpec((d, LANES), c2), pl.BlockSpec((LANES, d), c2),
                  pl.BlockSpec((LANES, LANES), c2)],
        out_specs=[pm_spec] * 10 + [pl.BlockSpec((tm, d), lambda i: (i, 0))],
        out_shape=[pm_shape] * 10 + [jax.ShapeDtypeStruct((n_rows, d), F32)],
        compiler_params=_cparams(("parallel",)),
        name="rw_in",
    )(h, h, h, mix, vecs, wr, wk, wv, w1, w2, a1, a2, g1, g2, bd)


def _rwkv_kernel(*refs):
    ins_f, ins_b = refs[0:6], refs[6:12]
    y_f, y_b, st = refs[12:]

    @pl.when(pl.program_id(1) == 0)
    def _():
        st[...] = jnp.zeros_like(st)

    L = C_CHUNK
    ri = _iota((L, L), 0)
    ci = _iota((L, L), 1)
    si = _iota((2 * L, 2 * L), 0) % L
    sj = _iota((2 * L, 2 * L), 1) % L
    incl = (si >= sj, si <= sj)
    strict = (si > sj, si < sj)
    eye = jnp.where(_iota((2 * L, 2 * L), 0) == _iota((2 * L, 2 * L), 1), 1.0, 0.0)
    lo_lane = _iota((L, LANES), 1) < C_HEAD

    def stack(x):
        return jnp.concatenate([jnp.where(lo_lane, x, 0.0), jnp.where(lo_lane, 0.0, x)], axis=0)

    refs_d = (ins_f, ins_b)
    y_refs = (y_f, y_b)
    for g0 in range(0, C_PAIRS, RW_GROUP):
        chains = [(d, p) for p in range(g0, g0 + RW_GROUP) for d in range(2)]
        n = len(chains)
        cs = [_cumsum_rows(refs_d[d][1][p], d == 1) for d, p in chains]
        big, vs, xa0r0, xbke, dec = [], [], [], [], []
        for (d, p), c in zip(chains, cs):
            r_ref, lw_ref, k_ref, v_ref, kk_ref, b_ref = refs_d[d]
            r, lw, k, v, b = r_ref[p], lw_ref[p], k_ref[p], v_ref[p], b_ref[p]
            a = -kk_ref[p]
            cex = c - lw
            if d == 0:
                mid, c_end = c[L // 2 - 1:L // 2, :], c[L - 1:L, :]
            else:
                mid, c_end = c[L // 2:L // 2 + 1, :], c[0:1, :]
            e_k = jnp.exp(mid - c)
            e_end = jnp.exp(c_end - c)
            lhs = jnp.concatenate([stack(a * jnp.exp(cex - mid)), stack(r * jnp.exp(c - mid))], axis=0)
            rhs = jnp.concatenate([stack(b * e_k), stack(k * e_k)], axis=0)
            big.append(_mm_nt(lhs, rhs))
            xa0r0.append(jnp.concatenate([stack(a * jnp.exp(cex)), stack(r * jnp.exp(c))], axis=0).astype(BF16))
            xbke.append(jnp.concatenate([stack(b * e_end), stack(k * e_end)], axis=0).astype(BF16))
            vs.append(stack(v))
            dec.append(jnp.exp(c_end))
        qs = [jnp.where(strict[d], bg[:2 * L, :2 * L], 0.0) for (d, _), bg in zip(chains, big)]
        n_ak = [jnp.where(strict[d], bg[:2 * L, 2 * L:], 0.0).astype(BF16) for (d, _), bg in zip(chains, big)]
        m_r = [jnp.where(jnp.concatenate([incl[d], incl[d]], axis=1), bg[2 * L:, :], 0.0).astype(BF16)
               for (d, _), bg in zip(chains, big)]
        ps = [eye + q for q in qs]
        for _ in range(5):
            qb = [q.astype(BF16) for q in qs]
            qs = [_mm(q, q) for q in qb]
            qb = [q.astype(BF16) for q in qs]
            ps = [pp + _mm(pp, q) for pp, q in zip(ps, qb)]
        s_prev = [st[d, p] for d, p in chains]
        sa = [_mm_nt(x, s) for x, s in zip(xa0r0, s_prev)]
        rhs = [sa[i][:2 * L] + _mm(n_ak[i], vs[i]) for i in range(n)]
        us = [_mm(ps[i], rhs[i]) for i in range(n)]
        uv = [jnp.concatenate([us[i], vs[i]], axis=0).astype(BF16) for i in range(n)]
        ys = [sa[i][2 * L:] + _mm(m_r[i], uv[i]) for i in range(n)]
        for i, (d, p) in enumerate(chains):
            st[d, p] = s_prev[i] * dec[i] + _mm_tn(uv[i], xbke[i])
            y_refs[d][p] = ys[i][:L] + ys[i][L:]


def _rwkv_scan(r, v, kk, lw, k, b, *, n_rows, t_lat, t_ctx, bsz):
    ch = C_CHUNK
    nlat, nctx = t_lat // ch, t_ctx // ch
    fwd = functools.partial(_scan_block, nlat=nlat, nctx=nctx, bsz=bsz, reverse=False)
    bwd = functools.partial(_scan_block, nlat=nlat, nctx=nctx, bsz=bsz, reverse=True)
    sf = pl.BlockSpec((C_PAIRS, ch, LANES), lambda bb, s: (0, fwd(bb, s), 0))
    sb = pl.BlockSpec((C_PAIRS, ch, LANES), lambda bb, s: (0, bwd(bb, s), 0))
    return pl.pallas_call(
        _rwkv_kernel,
        grid=(bsz, nlat + nctx),
        in_specs=[sf] * 6 + [sb] * 6,
        out_specs=[sf, sb],
        out_shape=[jax.ShapeDtypeStruct((C_PAIRS, n_rows, LANES), F32)] * 2,
        scratch_shapes=[pltpu.VMEM((2, C_PAIRS, LANES, LANES), F32)],
        compiler_params=_cparams(("parallel", "arbitrary")),
        name="rwkv_scan",
    )(r, lw[0], k[0], v, kk, b[0], r, lw[1], k[1], v, kk, b[1])


def _rw_out_kernel(x_ref, m_ref, yf, yb, bv, g_ref, lnw, lnb, bd_ref, w_ref, o_ref):
    bd = bd_ref[...]
    parts = []
    for p in range(C_PAIRS):
        cs = slice(p * LANES, (p + 1) * LANES)
        y = yf[p] + yb[p]
        mu = _mm_exact_r(y, bd) * (1.0 / C_HEAD)
        yc = y - mu
        var = _mm_exact_r(yc * yc, bd) * (1.0 / C_HEAD)
        yn = yc * lax.rsqrt(var + LNX_EPS) * lnw[:, cs] + lnb[:, cs] + bv[p]
        parts.append((yn * g_ref[:, cs]).astype(BF16))
    cat = jnp.concatenate(parts, axis=-1)
    o_ref[...] = x_ref[...] + m_ref[5:6, :] * _dot(cat, w_ref[...], ((1,), (0,)))


def _rw_out(x, mod, yf, yb, bv, g, lnw, lnb, bd, w, *, n_rows, t_lat, bsz):
    d = x.shape[1]
    tm = 256
    row = lambda i: (i, 0)
    const = lambda i: (0, 0)
    pm = pl.BlockSpec((C_PAIRS, tm, LANES), lambda i: (0, i, 0))
    return pl.pallas_call(
        _rw_out_kernel,
        grid=(n_rows // tm,),
        in_specs=[pl.BlockSpec((tm, d), row),
                  pl.BlockSpec((None, N_ADA, d), lambda i: (_mod_row(i, tm, t_lat, bsz), 0, 0)),
                  pm, pm, pm, pl.BlockSpec((tm, d), row),
                  pl.BlockSpec((1, d), const), pl.BlockSpec((1, d), const),
                  pl.BlockSpec((LANES, LANES), const), pl.BlockSpec((d, d), const)],
        out_specs=pl.BlockSpec((tm, d), row),
        out_shape=jax.ShapeDtypeStruct((n_rows, d), F32),
        compiler_params=_cparams(("parallel",)),
        name="rw_out",
    )(x, mod, yf, yb, bv, g, lnw.reshape(1, d), lnb.reshape(1, d), bd, w)


def _even_layer(x, mod, g, wi, wo, layer, in_w, out_w, lb, hgrn_g, conv_w, conv_b, dt_bias, a_log, d_skip, ssd_g, dims):
    n_all, t_lat, t_ctx, bsz = dims["n_all"], dims["t_lat"], dims["t_ctx"], dims["bsz"]
    n_lat = bsz * t_lat
    x, h = _ffn_half(x, mod, g[0], wi, wo, (layer, 0), slot=0, n_rows=n_all, t_lat=t_lat, bsz=bsz,
                     post="prenorm", g2=g[1], h_dtype=BF16)
    w_pad = jnp.pad(in_w, ((0, 0), (0, EVEN_IN_PAD - EVEN_IN))).astype(BF16)
    dtb = jnp.pad(dt_bias.reshape(1, 2 * B_HEADS), ((0, 0), (0, LANES - 2 * B_HEADS)))
    qi, f, gz, xbc, dt = _even_in(h, w_pad, lb.reshape(1, 2 * A_DIM), dtb, n_all)
    xbc = _conv_silu(xbc, conv_w, conv_b, n_rows=n_all, t_lat=t_lat, t_ctx=t_ctx, n_lat=n_lat)
    oaf, oab = _hgrn_scan(qi, f, n_rows=n_all, t_lat=t_lat, t_ctx=t_ctx, bsz=bsz)
    a_row = jnp.pad(-jnp.exp(a_log.astype(F32)).reshape(1, 2 * B_HEADS), ((0, 0), (0, LANES - 2 * B_HEADS)))
    dsk_row = jnp.repeat(d_skip.astype(F32), B_HEADDIM).reshape(1, B_DIM)
    e_np = np.zeros((2, LANES, B_DIM), np.float32)
    for dd in range(2):
        for hh in range(B_HEADS):
            e_np[dd, dd * B_HEADS + hh, hh * B_HEADDIM:(hh + 1) * B_HEADDIM] = 1.0
    ybf, ybb = _ssd_scan(xbc, dt, a_row, dsk_row, jnp.asarray(e_np, BF16),
                         n_rows=n_all, t_lat=t_lat, t_ctx=t_ctx, bsz=bsz)
    x = _even_out(x, mod, oaf, oab, ybf, ybb, gz, hgrn_g.reshape(1, A_DK), ssd_g.reshape(1, B_DIM),
                  out_w.astype(BF16), n_rows=n_all, t_lat=t_lat, bsz=bsz)
    return x


def _odd_layer(x, mod, g, wi, wo, layer, p, dims):
    n_all, t_lat, t_ctx, bsz = dims["n_all"], dims["t_lat"], dims["t_ctx"], dims["bsz"]
    n_lat = bsz * t_lat
    d = D_MODEL
    x, h = _ffn_half(x, mod, g[0], wi, wo, (layer, 0), slot=0, n_rows=n_all, t_lat=t_lat, bsz=bsz,
                     post="prenorm", g2=g[1], h_dtype=F32)
    vecs = jnp.concatenate([p["w0"], p["a0"], p["k_k"][None], p["k_a"][None], p["r_k"].reshape(1, d),
                            jnp.zeros((1, d), F32)], axis=0)
    w1 = jnp.concatenate([p["w1"][0], p["w1"][1]], axis=1).astype(BF16)
    a1 = jnp.concatenate([p["a1"][0], p["a1"][1]], axis=1).astype(BF16)
    bd_np = np.kron(np.eye(2, dtype=np.float32), np.ones((C_HEAD, C_HEAD), np.float32))
    bd = jnp.asarray(bd_np, BF16)
    outs = _rw_in(h, p["x_mix"], vecs, p["wr"].astype(BF16), p["wk"].astype(BF16), p["wv"].astype(BF16),
                  w1, p["w2"].astype(BF16), a1, p["a2"].astype(BF16), p["g1"].astype(BF16),
                  p["g2"].astype(BF16), bd, n_rows=n_all, t_lat=t_lat, t_ctx=t_ctx, n_lat=n_lat)
    r, v, kk, bv, lw0, lw1, k0, k1, b0, b1, gg = outs
    yf, yb = _rwkv_scan(r, v, kk, (lw0, lw1), (k0, k1), (b0, b1), n_rows=n_all, t_lat=t_lat, t_ctx=t_ctx, bsz=bsz)
    x = _rw_out(x, mod, yf, yb, bv, gg, p["ln_w"], p["ln_b"], bd, p["wo"].astype(BF16),
                n_rows=n_all, t_lat=t_lat, bsz=bsz)
    return x


def kernel(x, c, ctx, c_ctx, ada_w, ada_b, norm_g, ffn_wi, ffn_wo, final_g, hgrn_lb_logits, ev_in_w, ev_out_w, hgrn_norm_g, ssd_conv_w, ssd_conv_b, ssd_dt_bias, ssd_a_log, ssd_d, ssd_norm_g, rw_x_mix, rw_wr, rw_wk, rw_wv, rw_wo, rw_w0, rw_w1, rw_w2, rw_a0, rw_a1, rw_a2, rw_g1, rw_g2, rw_k_k, rw_k_a, rw_r_k, rw_ln_w, rw_ln_b):
    bsz, t_lat, d = x.shape
    t_ctx = ctx.shape[1]
    depth = ada_w.shape[0]
    n_lat = bsz * t_lat
    n_all = n_lat + bsz * t_ctx
    dims = dict(n_all=n_all, t_lat=t_lat, t_ctx=t_ctx, bsz=bsz)
    assert bsz + 1 <= 8

    xs = (x.reshape(n_lat, d), ctx.reshape(bsz * t_ctx, d))
    cvec = jnp.concatenate([c, c_ctx[None], jnp.zeros((8 - bsz - 1, d), F32)], axis=0)
    mod_all = _modulation(cvec, ada_w, ada_b).reshape(depth, 8, N_ADA, d)
    lb_all = jnp.cumsum(jax.nn.softmax(hgrn_lb_logits.astype(F32), axis=0), axis=0)
    wi_bf = ffn_wi.astype(BF16)
    wo_bf = ffn_wo.astype(BF16)

    for layer in range(depth):
        last = layer == depth - 1
        mod = mod_all[layer]
        g = norm_g[layer]
        if layer % 2 == 0:
            e = layer // 2
            xs = _even_layer(xs, mod, g, wi_bf, wo_bf, layer, ev_in_w[e], ev_out_w[e], lb_all[layer],
                             hgrn_norm_g[e], ssd_conv_w[e], ssd_conv_b[e], ssd_dt_bias[e], ssd_a_log[e],
                             ssd_d[e], ssd_norm_g[e], dims)
        else:
            o = layer // 2
            p = dict(x_mix=rw_x_mix[o], wr=rw_wr[o], wk=rw_wk[o], wv=rw_wv[o], wo=rw_wo[o], w0=rw_w0[o],
                     w1=rw_w1[o], w2=rw_w2[o], a0=rw_a0[o], a1=rw_a1[o], a2=rw_a2[o], g1=rw_g1[o], g2=rw_g2[o],
                     k_k=rw_k_k[o], k_a=rw_k_a[o], r_k=rw_r_k[o], ln_w=rw_ln_w[o], ln_b=rw_ln_b[o])
            xs = _odd_layer(xs, mod, g, wi_bf, wo_bf, layer, p, dims)
        if last:
            out = _ffn_half(xs, mod, g[2], wi_bf, wo_bf, (layer, 1), slot=2, n_rows=n_lat, t_lat=t_lat,
                            bsz=bsz, post="final", g2=final_g)
        else:
            xs = _ffn_half(xs, mod, g[2], wi_bf, wo_bf, (layer, 1), slot=2, n_rows=n_all, t_lat=t_lat,
                           bsz=bsz)
    return out.reshape(bsz, t_lat, d)
```

```python
import functools

import numpy as np
import jax
import jax.numpy as jnp
from jax import lax
from jax.experimental import pallas as pl
from jax.experimental.pallas import tpu as pltpu

F32 = jnp.float32
BF16 = jnp.bfloat16

D_MODEL = 1024
N_ADA = 9
FF_DIM = 2816
EPS = 1e-6
GRID_W = 64
A_HEADS = 4
A_DK = 128
A_DIM = A_HEADS * A_DK
A_CHUNK = 32
B_HEADS = 8
B_HEADDIM = 64
B_DIM = B_HEADS * B_HEADDIM
B_GROUPS = 2
B_STATE = 128
B_CHUNK = 128
B_XBC = B_DIM + 2 * B_GROUPS * B_STATE
EVEN_IN = A_DIM * 5 + B_DIM + B_XBC + 2 * B_HEADS
EVEN_IN_PAD = 4224
C_HEAD = 64
C_HEADS = D_MODEL // C_HEAD
C_PAIRS = C_HEADS // 2
C_CHUNK = 64
LNX_EPS = 64e-5
RW_GROUP = 8
LANES = 128

VMEM_LIMIT = 56 * 1024 * 1024


def _cparams(sem):
    return pltpu.CompilerParams(dimension_semantics=sem, vmem_limit_bytes=VMEM_LIMIT)


def _sigmoid(x):
    return 1.0 / (1.0 + jnp.exp(-x))


def _silu(x):
    return x * _sigmoid(x)


def _softplus(x):
    return jnp.maximum(x, 0.0) + jnp.log1p(jnp.exp(-jnp.abs(x)))


def _dot(a, b, dims):
    return lax.dot_general(a, b, (dims, ((), ())), preferred_element_type=F32)


def _mm(a, b):
    return _dot(a.astype(BF16), b.astype(BF16), ((1,), (0,)))


def _mm_nt(a, b):
    return _dot(a.astype(BF16), b.astype(BF16), ((1,), (1,)))


def _mm_tn(a, b):
    return _dot(a.astype(BF16), b.astype(BF16), ((0,), (0,)))


def _split2(x):
    hi = x.astype(BF16)
    lo = (x - hi.astype(F32)).astype(BF16)
    return hi, lo


def _split3(x):
    x1 = x.astype(BF16)
    r1 = x - x1.astype(F32)
    x2 = r1.astype(BF16)
    x3 = (r1 - x2.astype(F32)).astype(BF16)
    return x1, x2, x3


def _cumsum_rows(x, reverse):
    n = x.shape[0]
    row = _iota(x.shape, 0)
    s = 1
    while s < n:
        if reverse:
            x = x + jnp.where(row < n - s, pltpu.roll(x, n - s, 0), 0.0)
        else:
            x = x + jnp.where(row >= s, pltpu.roll(x, s, 0), 0.0)
        s *= 2
    return x


def _mm_exact_r(x, m_bf):
    x1, x2, x3 = _split3(x)
    dn = ((1,), (0,))
    return _dot(x1, m_bf, dn) + _dot(x2, m_bf, dn) + _dot(x3, m_bf, dn)


def _mm_hp(a, b):
    ah, al = _split2(a)
    bh, bl = _split2(b)
    dn = ((1,), (0,))
    return _dot(ah, bh, dn) + _dot(ah, bl, dn) + _dot(al, bh, dn)


def _rms(x):
    return x * lax.rsqrt(jnp.mean(x * x, axis=-1, keepdims=True) + EPS)


def _iota(shape, dim):
    return lax.broadcasted_iota(jnp.int32, shape, dim)


def _mod_row(i, tm, t_lat, bsz):
    return jnp.minimum((i * tm) // t_lat, bsz)


def _seq_pos(r0, t_lat, t_ctx, n_lat):
    in_lat = r0 < n_lat
    pos = jnp.where(in_lat, r0 % t_lat, (r0 - n_lat) % t_ctx)
    slen = jnp.where(in_lat, t_lat, t_ctx)
    return in_lat, pos, slen


def _scan_block(b, s, nlat, nctx, bsz, reverse):
    if reverse:
        cblk = bsz * nlat + b * nctx + (nctx - 1 - s)
        lblk = b * nlat + (nlat - 1 - (s - nctx))
    else:
        cblk = bsz * nlat + b * nctx + s
        lblk = b * nlat + (s - nctx)
    return jnp.where(s < nctx, cblk, lblk)


def _mod_kernel(c_ref, w_ref, b_ref, o_ref):
    sc = _silu(c_ref[...])
    o_ref[...] = _mm(sc, w_ref[...]) + b_ref[...]


def _modulation(cvec, ada_w, ada_b):
    depth, d, n = ada_w.shape
    tn = 1536
    return pl.pallas_call(
        _mod_kernel,
        grid=(depth, n // tn),
        in_specs=[pl.BlockSpec((8, d), lambda l, j: (0, 0)),
                  pl.BlockSpec((None, d, tn), lambda l, j: (l, 0, j)),
                  pl.BlockSpec((None, 1, tn), lambda l, j: (l, 0, j))],
        out_specs=pl.BlockSpec((None, 8, tn), lambda l, j: (l, 0, j)),
        out_shape=jax.ShapeDtypeStruct((depth, 8, n), F32),
        compiler_params=_cparams(("arbitrary", "arbitrary")),
        name="ada_mod",
    )(cvec, ada_w, ada_b.reshape(depth, 1, n))


def _ffn_kernel(*refs, slot, tf, post, n_lat_tiles):
    if n_lat_tiles is None:
        x_ref = refs[0]
        x = x_ref[...]
    else:
        x = jnp.where(pl.program_id(0) >= n_lat_tiles, refs[1][...], refs[0][...])
        refs = refs[1:]
    m_ref, g_ref, wi_ref, wo_ref = refs[1:5]
    if post == "prenorm":
        g2_ref, o_ref, h_ref, acc = refs[5:]
    elif post == "final":
        g2_ref, o_ref, acc = refs[5:]
    else:
        o_ref, acc = refs[5:]
    dn = ((1,), (0,))
    nch = FF_DIM // tf
    h = _rms(x) * g_ref[...] * (1.0 + m_ref[3 * slot + 1:3 * slot + 2, :]) + m_ref[3 * slot:3 * slot + 1, :]
    h = h.astype(BF16)

    def gate_up(c):
        return (_dot(h, wi_ref[:, c * tf:(c + 1) * tf], dn),
                _dot(h, wi_ref[:, FF_DIM + c * tf:FF_DIM + (c + 1) * tf], dn))

    nxt = gate_up(0)
    for c in range(nch):
        gt, up = nxt
        if c + 1 < nch:
            nxt = gate_up(c + 1)
        part = _dot((_silu(gt) * up).astype(BF16), wo_ref[c * tf:(c + 1) * tf, :], dn)
        if c == 0:
            acc[...] = part
        else:
            acc[...] += part

    xn = x + 0.5 * m_ref[3 * slot + 2:3 * slot + 3, :] * acc[...]
    if post == "final":
        o_ref[...] = _rms(xn) * g2_ref[...]
    else:
        o_ref[...] = xn
    if post == "prenorm":
        h2 = _rms(xn) * g2_ref[...] * (1.0 + m_ref[4:5, :]) + m_ref[3:4, :]
        h_ref[...] = h2.astype(h_ref.dtype)


def _ffn_half(x, mod, g, wi_all, wo_all, sel, *, slot, n_rows, t_lat, bsz, post=None, g2=None, h_dtype=F32):
    split = isinstance(x, tuple)
    d = D_MODEL
    tm = 512
    while n_rows % tm or t_lat % tm:
        tm //= 2
    const = lambda i: (0, 0)
    resident = pl.Buffered(1)
    if split:
        n_lat_tiles = x[0].shape[0] // tm
        x_specs = [pl.BlockSpec((tm, d), lambda i: (jnp.minimum(i, n_lat_tiles - 1), 0)),
                   pl.BlockSpec((tm, d), lambda i: (jnp.maximum(i - n_lat_tiles, 0), 0))]
        x_args = list(x)
    else:
        n_lat_tiles = None
        x_specs = [pl.BlockSpec((tm, d), lambda i: (i, 0))]
        x_args = [x]
    in_specs = x_specs + [
        pl.BlockSpec((None, N_ADA, d), lambda i: (_mod_row(i, tm, t_lat, bsz), 0, 0)),
        pl.BlockSpec((1, d), const),
        pl.BlockSpec((None, None, d, 2 * FF_DIM), lambda i: sel + (0, 0), pipeline_mode=resident),
        pl.BlockSpec((None, None, FF_DIM, d), lambda i: sel + (0, 0), pipeline_mode=resident)]
    args = x_args + [mod, g.reshape(1, d), wi_all, wo_all]
    out_spec = pl.BlockSpec((tm, d), lambda i: (i, 0))
    out_shape = jax.ShapeDtypeStruct((n_rows, d), F32)
    if post is not None:
        in_specs.append(pl.BlockSpec((1, d), const))
        args.append(g2.reshape(1, d))
    if post == "prenorm":
        out_specs = [out_spec, out_spec]
        out_shapes = [out_shape, jax.ShapeDtypeStruct((n_rows, d), h_dtype)]
    else:
        out_specs, out_shapes = out_spec, out_shape
    return pl.pallas_call(
        functools.partial(_ffn_kernel, slot=slot, tf=256, post=post, n_lat_tiles=n_lat_tiles),
        grid=(n_rows // tm,),
        in_specs=in_specs,
        out_specs=out_specs,
        out_shape=out_shapes,
        scratch_shapes=[pltpu.VMEM((tm, d), F32)],
        compiler_params=_cparams(("parallel",)),
        name="ffn_half",
    )(*args)


HALO = 16


def _even_in_kernel(h_ref, hp_ref, hn_ref, w_ref, lb_ref, dtb_ref, cw_ref, cb_ref,
                    qi_ref, f_ref, gz_ref, xbc_ref, dt_ref, *, tm, t_lat, t_ctx, n_lat):
    h = h_ref[...]
    dn = ((1,), (0,))
    qi = _dot(h, w_ref[:, 0:1024], dn)
    qi_ref[:, 0:512] = _silu(qi[:, 0:512]).astype(qi_ref.dtype)
    qi_ref[:, 512:1024] = qi[:, 512:1024].astype(qi_ref.dtype)
    lb = lb_ref[...]
    f_ref[...] = lb + (1.0 - lb) * _sigmoid(_dot(h, w_ref[:, 1024:2048], dn))
    gz_ref[...] = _silu(_dot(h, w_ref[:, 2048:3072], dn)).astype(gz_ref.dtype)
    dt_ref[...] = _softplus(_dot(h, w_ref[:, 4096:4224], dn) + dtb_ref[...])
    n_ext = tm + 2 * HALO
    xe = _dot(jnp.concatenate([hp_ref[...], h, hn_ref[...]], axis=0), w_ref[:, 3072:4096], dn)
    x = xe[HALO:HALO + tm]
    xp = pltpu.roll(xe, 1, 0)[HALO:HALO + tm]
    xn = pltpu.roll(xe, n_ext - 1, 0)[HALO:HALO + tm]
    r = pl.program_id(0) * tm + _iota((tm, 1), 0)
    in_lat = r < n_lat
    pos = jnp.where(in_lat, r % t_lat, (r - n_lat) % t_ctx)
    slen = jnp.where(in_lat, t_lat, t_ctx)
    xp = jnp.where(pos == 0, 0.0, xp)
    xn = jnp.where(pos == slen - 1, 0.0, xn)
    y = cw_ref[0:1, :] * xp + cw_ref[1:2, :] * x + cw_ref[2:3, :] * xn + cb_ref[...]
    xbc_ref[...] = _silu(y).astype(xbc_ref.dtype)


def _even_in(h, w, lb, dtb, conv_w, conv_b, *, n_rows, t_lat, t_ctx, n_lat):
    d = h.shape[1]
    tm = 512
    while n_rows % tm:
        tm //= 2
    nbh = n_rows // HALO
    widths = (1024, 1024, 1024, B_XBC, LANES)
    dtypes = (BF16, F32, BF16, BF16, F32)
    const = lambda i: (0, 0)
    return pl.pallas_call(
        functools.partial(_even_in_kernel, tm=tm, t_lat=t_lat, t_ctx=t_ctx, n_lat=n_lat),
        grid=(n_rows // tm,),
        in_specs=[pl.BlockSpec((tm, d), lambda i: (i, 0)),
                  pl.BlockSpec((HALO, d), lambda i: (jnp.maximum(i * (tm // HALO) - 1, 0), 0)),
                  pl.BlockSpec((HALO, d), lambda i: (jnp.minimum((i + 1) * (tm // HALO), nbh - 1), 0)),
                  pl.BlockSpec((d, EVEN_IN_PAD), const, pipeline_mode=pl.Buffered(1)),
                  pl.BlockSpec((1, 1024), const),
                  pl.BlockSpec((1, LANES), const),
                  pl.BlockSpec((3, B_XBC), const),
                  pl.BlockSpec((1, B_XBC), const)],
        out_specs=[pl.BlockSpec((tm, wd), lambda i: (i, 0)) for wd in widths],
        out_shape=[jax.ShapeDtypeStruct((n_rows, wd), dt) for wd, dt in zip(widths, dtypes)],
        compiler_params=_cparams(("parallel",)),
        name="even_in",
    )(h, h, h, w, lb, dtb, conv_w, conv_b.reshape(1, B_XBC))


def _hgrn_kernel(qi_f, f_f, qi_b, f_b, o_f, o_b, st, *, ch):
    @pl.when(pl.program_id(1) == 0)
    def _():
        st[...] = jnp.zeros_like(st)

    L = A_CHUNK
    nsub = ch // L
    ri = _iota((L, L), 0)
    ci = _iota((L, L), 1)
    incl = (ri >= ci, ri <= ci)
    qis, fs, outs = (qi_f, qi_b), (f_f, f_b), (o_f, o_b)

    chains = [(d, h) for h in range(A_HEADS) for d in range(2)]

    def rows_of(j, d):
        jj = j if d == 0 else nsub - 1 - j
        return slice(jj * L, (jj + 1) * L)

    def local_stage(j):
        out = []
        for d, h in chains:
            cs = slice(h * A_DK, (h + 1) * A_DK)
            rows = rows_of(j, d)
            q = qis[d][rows, cs].astype(F32)
            v = qis[d][rows, A_DIM + h * A_DK:A_DIM + (h + 1) * A_DK].astype(BF16)
            f = fs[d][rows, cs]
            k = 1.0 - f
            bc = _cumsum_rows(jnp.log(f), d == 1)
            if d == 0:
                blast, bref = bc[L - 1:L, :], bc[L // 2 - 1:L // 2, :]
            else:
                blast, bref = bc[0:1, :], bc[L // 2:L // 2 + 1, :]
            out.append(dict(
                v=v, dec=jnp.exp(blast), qe=(q * jnp.exp(bc)).astype(BF16),
                ds=_mm_tn(v, k * jnp.exp(blast - bc)),
                sc=_mm_nt(q * jnp.exp(bc - bref), k * jnp.exp(bref - bc))))
        return out

    states = [st[d, h] for d, h in chains]
    nxt = local_stage(0)
    for j in range(nsub):
        cur = nxt
        if j + 1 < nsub:
            nxt = local_stage(j + 1)
        o2s = [_mm_nt(c["qe"], s) for c, s in zip(cur, states)]
        states = [c["dec"] * s + c["ds"] for c, s in zip(cur, states)]
        o1s = [_mm(jnp.where(incl[d], c["sc"], 0.0), c["v"]) for (d, _), c in zip(chains, cur)]
        for i, (d, h) in enumerate(chains):
            outs[d][rows_of(j, d), h * A_DK:(h + 1) * A_DK] = (o1s[i] + o2s[i]).astype(outs[d].dtype)
    for i, (d, h) in enumerate(chains):
        st[d, h] = states[i]


def _hgrn_scan(qi, f, *, n_rows, t_lat, t_ctx, bsz):
    ch = 256
    nlat, nctx = t_lat // ch, t_ctx // ch
    fwd = functools.partial(_scan_block, nlat=nlat, nctx=nctx, bsz=bsz, reverse=False)
    bwd = functools.partial(_scan_block, nlat=nlat, nctx=nctx, bsz=bsz, reverse=True)
    return pl.pallas_call(
        functools.partial(_hgrn_kernel, ch=ch),
        grid=(bsz, nlat + nctx),
        in_specs=[pl.BlockSpec((ch, 2 * A_DIM), lambda b, s: (fwd(b, s), 0)),
                  pl.BlockSpec((ch, A_DIM), lambda b, s: (fwd(b, s), 0)),
                  pl.BlockSpec((ch, 2 * A_DIM), lambda b, s: (bwd(b, s), 0)),
                  pl.BlockSpec((ch, A_DIM), lambda b, s: (bwd(b, s), 1))],
        out_specs=[pl.BlockSpec((ch, A_DIM), lambda b, s: (fwd(b, s), 0)),
                   pl.BlockSpec((ch, A_DIM), lambda b, s: (bwd(b, s), 0))],
        out_shape=[jax.ShapeDtypeStruct((n_rows, A_DIM), BF16)] * 2,
        scratch_shapes=[pltpu.VMEM((2, A_HEADS, A_DK, A_DK), F32)],
        compiler_params=_cparams(("parallel", "arbitrary")),
        name="hgrn_scan",
    )(qi, f, qi, f)


def _ssd_kernel(x_f, dt_f, x_b, dt_b, a_ref, dsk_ref, e_ref, o_f, o_b, st):
    @pl.when(pl.program_id(1) == 0)
    def _():
        st[...] = jnp.zeros_like(st)

    L = B_CHUNK
    hg = B_HEADS // B_GROUPS
    gw = hg * B_HEADDIM
    ri = _iota((L, L), 0)
    ci = _iota((L, L), 1)
    incl = (ri >= ci, ri <= ci)
    xs, dts, outs = (x_f, x_b), (dt_f, dt_b), (o_f, o_b)
    dt = [dts[d][...] for d in range(2)]
    acum = [_cumsum_rows(dt[d] * a_ref[...], d == 1) for d in range(2)]
    acum_e = [_mm_exact_r(acum[d], e_ref[d]) for d in range(2)]
    dt_e = [_mm_exact_r(dt[d], e_ref[d]) for d in range(2)]
    alast_e = [acum_e[0][L - 1:L, :], acum_e[1][0:1, :]]
    acum_t = [acum[d].T for d in range(2)]
    xa = [xs[d][:, 0:B_DIM].astype(F32) for d in range(2)]
    xdt = [xa[d] * dt_e[d] for d in range(2)]
    xdec = [xdt[d] * jnp.exp(alast_e[d] - acum_e[d]) for d in range(2)]
    dg = [(d, g) for d in range(2) for g in range(B_GROUPS)]
    bgs = [xs[d][:, B_DIM + g * B_STATE:B_DIM + (g + 1) * B_STATE].astype(BF16) for d, g in dg]
    cgs = [xs[d][:, B_DIM + (B_GROUPS + g) * B_STATE:B_DIM + (B_GROUPS + g + 1) * B_STATE].astype(BF16)
           for d, g in dg]
    cbs = [_mm_nt(cgs[i], bgs[i]) for i in range(len(dg))]
    s_prev = [st[d, g] for d, g in dg]
    y_off = [_mm(cgs[i], s_prev[i]) * jnp.exp(acum_e[d][:, g * gw:(g + 1) * gw]) for i, (d, g) in enumerate(dg)]
    for i, (d, g) in enumerate(dg):
        gs = slice(g * gw, (g + 1) * gw)
        st[d, g] = s_prev[i] * jnp.exp(alast_e[d][:, gs]) + _mm_tn(bgs[i], xdec[d][:, gs])
    ys = []
    for i, (d, g) in enumerate(dg):
        for hh in range(hg):
            h = g * hg + hh
            j = d * B_HEADS + h
            seg = acum[d][:, j:j + 1] - acum_t[d][j:j + 1, :]
            lm = jnp.exp(jnp.where(incl[d], seg, -1e30))
            ys.append(_mm(cbs[i] * lm, xdt[d][:, h * B_HEADDIM:(h + 1) * B_HEADDIM]))
    for i, (d, g) in enumerate(dg):
        for hh in range(hg):
            h = g * hg + hh
            hs = slice(h * B_HEADDIM, (h + 1) * B_HEADDIM)
            y = ys[i * hg + hh] + y_off[i][:, hh * B_HEADDIM:(hh + 1) * B_HEADDIM]
            if d == 0:
                y = y + dsk_ref[:, hs] * xa[d][:, hs]
            outs[d][:, hs] = y.astype(outs[d].dtype)


def _ssd_scan(xbc, dt, a_row, dsk_row, e_mat, *, n_rows, t_lat, t_ctx, bsz):
    ch = B_CHUNK
    nlat, nctx = t_lat // ch, t_ctx // ch
    fwd = functools.partial(_scan_block, nlat=nlat, nctx=nctx, bsz=bsz, reverse=False)
    bwd = functools.partial(_scan_block, nlat=nlat, nctx=nctx, bsz=bsz, reverse=True)
    const2 = lambda b, s: (0, 0)
    return pl.pallas_call(
        _ssd_kernel,
        grid=(bsz, nlat + nctx),
        in_specs=[pl.BlockSpec((ch, B_XBC), lambda b, s: (fwd(b, s), 0)),
                  pl.BlockSpec((ch, LANES), lambda b, s: (fwd(b, s), 0)),
                  pl.BlockSpec((ch, B_XBC), lambda b, s: (bwd(b, s), 0)),
                  pl.BlockSpec((ch, LANES), lambda b, s: (bwd(b, s), 0)),
                  pl.BlockSpec((1, LANES), const2),
                  pl.BlockSpec((1, B_DIM), const2),
                  pl.BlockSpec((2, LANES, B_DIM), lambda b, s: (0, 0, 0))],
        out_specs=[pl.BlockSpec((ch, B_DIM), lambda b, s: (fwd(b, s), 0)),
                   pl.BlockSpec((ch, B_DIM), lambda b, s: (bwd(b, s), 0))],
        out_shape=[jax.ShapeDtypeStruct((n_rows, B_DIM), BF16)] * 2,
        scratch_shapes=[pltpu.VMEM((2, B_GROUPS, B_STATE, (B_HEADS // B_GROUPS) * B_HEADDIM), F32)],
        compiler_params=_cparams(("parallel", "arbitrary")),
        name="ssd_scan",
    )(xbc, dt, xbc, dt, a_row, dsk_row, e_mat)


def _even_out_kernel(x_ref, m_ref, oaf, oab, ybf, ybb, gz_ref, ga_ref, gb_ref, w_ref, o_ref):
    oa = oaf[...].astype(F32) + oab[...].astype(F32)
    gz = gz_ref[...].astype(F32)
    parts = []
    for h in range(A_HEADS):
        cs = slice(h * A_DK, (h + 1) * A_DK)
        parts.append(_rms(oa[:, cs]) * ga_ref[...] * gz[:, cs])
    ob = (ybf[...].astype(F32) + ybb[...].astype(F32)) * gz[:, A_DIM:A_DIM + B_DIM]
    gw = B_DIM // B_GROUPS
    for g in range(B_GROUPS):
        cs = slice(g * gw, (g + 1) * gw)
        parts.append(_rms(ob[:, cs]) * gb_ref[:, cs])
    cat = jnp.concatenate(parts, axis=-1).astype(BF16)
    y = _dot(cat, w_ref[...], ((1,), (0,)))
    o_ref[...] = x_ref[...] + m_ref[5:6, :] * y


def _even_out(x, mod, oaf, oab, ybf, ybb, gz, ga, gb, w, *, n_rows, t_lat, bsz):
    d = x.shape[1]
    tm = 512
    while n_rows % tm or t_lat % tm:
        tm //= 2
    row = lambda i: (i, 0)
    const = lambda i: (0, 0)
    return pl.pallas_call(
        _even_out_kernel,
        grid=(n_rows // tm,),
        in_specs=[pl.BlockSpec((tm, d), row),
                  pl.BlockSpec((None, N_ADA, d), lambda i: (_mod_row(i, tm, t_lat, bsz), 0, 0)),
                  pl.BlockSpec((tm, A_DIM), row), pl.BlockSpec((tm, A_DIM), row),
                  pl.BlockSpec((tm, B_DIM), row), pl.BlockSpec((tm, B_DIM), row),
                  pl.BlockSpec((tm, A_DIM + B_DIM), row),
                  pl.BlockSpec((1, A_DK), const), pl.BlockSpec((1, B_DIM), const),
                  pl.BlockSpec((A_DIM + B_DIM, d), const)],
        out_specs=pl.BlockSpec((tm, d), row),
        out_shape=jax.ShapeDtypeStruct((n_rows, d), F32),
        compiler_params=_cparams(("parallel",)),
        name="even_out",
    )(x, mod, oaf, oab, ybf, ybb, gz, ga, gb, w)


def _rw_in_kernel(h_ref, up_ref, dn_ref, mix_ref, vec_ref, wr, wk, wv, w1, w2, a1, a2, g1, g2, bd_ref,
                  r_o, v_o, kk_o, bv_o, lw0_o, lw1_o, k0_o, k1_o, b0_o, b1_o, g_o,
                  *, tm, t_lat, t_ctx, n_lat):
    d = D_MODEL
    r0 = pl.program_id(0) * tm
    in_lat, pos, slen = _seq_pos(r0, t_lat, t_ctx, n_lat)
    first = pos == 0
    last = pos + tm == slen
    row = _iota((tm, 1), 0)
    col = row % GRID_W
    ok_prev = jnp.where(in_lat, col, row) != 0
    ok_next = jnp.where(in_lat, col - (GRID_W - 1), row - (tm - 1)) != 0
    q4 = d // 4

    def prev_of(x):
        return jnp.where(ok_prev, pltpu.roll(x, 1, 0), 0.0)

    def next_of(x):
        return jnp.where(ok_next, pltpu.roll(x, tm - 1, 0), 0.0)

    h = h_ref[...]
    h2, h3 = h[:, 2 * q4:3 * q4], h[:, 3 * q4:]
    up = jnp.concatenate([jnp.where(first, 0.0, up_ref[:, 2 * q4:3 * q4]), h2[:tm - GRID_W]], axis=0)
    down = jnp.concatenate([h3[GRID_W:], jnp.where(last, 0.0, dn_ref[:, 3 * q4:])], axis=0)
    hs = jnp.concatenate([prev_of(h[:, :q4]), next_of(h[:, q4:2 * q4]),
                          jnp.where(in_lat, up, prev_of(h2)),
                          jnp.where(in_lat, down, next_of(h3))], axis=1)
    xx = hs - h
    xr, xw, xk, xv, xa, xg = [(h + xx * mix_ref[j:j + 1, :]).astype(BF16) for j in range(6)]
    dn = ((1,), (0,))
    r = _dot(xr, wr[...], dn)
    k = _dot(xk, wk[...], dn)
    v = _dot(xv, wv[...], dn)
    g = _mm(_sigmoid(_dot(xg, g1[...], dn)), g2[...])
    tw = jnp.tanh(_dot(xw, w1[...], dn))
    ta = _dot(xa, a1[...], dn)
    w0, a0 = vec_ref[0:2, :], vec_ref[2:4, :]
    k_k, k_a, r_k = vec_ref[4:5, :], vec_ref[5:6, :], vec_ref[6:7, :]
    bd = bd_ref[...]

    lws, ks, bs, kks = [], [], [], []
    kkr = k * k_k
    for p in range(C_PAIRS):
        cs = slice(p * LANES, (p + 1) * LANES)
        ss = _mm_exact_r(kkr[:, cs] * kkr[:, cs], bd)
        kks.append(kkr[:, cs] * lax.rsqrt(jnp.maximum(ss, 1e-24)))
    for dd in range(2):
        ls = slice(dd * 64, (dd + 1) * 64)
        zw = w0[dd:dd + 1, :] + _mm(tw[:, ls], w2[dd])
        lws.append(-np.float32(np.exp(-0.5)) * _sigmoid(zw))
        asig = _sigmoid(a0[dd:dd + 1, :] + _mm(ta[:, ls], a2[dd]))
        ks.append(k * (1.0 + (asig - 1.0) * k_a))
        bs.append(asig)
    rk = r * (ks[0] + ks[1]) * r_k
    g_o[...] = g.astype(g_o.dtype)
    for p in range(C_PAIRS):
        cs = slice(p * LANES, (p + 1) * LANES)
        r_o[p] = r[:, cs].astype(r_o.dtype)
        v_o[p] = v[:, cs].astype(v_o.dtype)
        kk_o[p] = kks[p].astype(kk_o.dtype)
        bv_o[p] = (_mm_exact_r(rk[:, cs], bd) * v[:, cs]).astype(bv_o.dtype)
        lw0_o[p] = lws[0][:, cs]
        lw1_o[p] = lws[1][:, cs]
        k0_o[p] = ks[0][:, cs].astype(k0_o.dtype)
        k1_o[p] = ks[1][:, cs].astype(k1_o.dtype)
        b0_o[p] = (kks[p] * bs[0][:, cs]).astype(b0_o.dtype)
        b1_o[p] = (kks[p] * bs[1][:, cs]).astype(b1_o.dtype)


def _rw_in(h, mix, vecs, wr, wk, wv, w1, w2, a1, a2, g1, g2, bd, *, n_rows, t_lat, t_ctx, n_lat):
    d = h.shape[1]
    tm = 256
    assert t_ctx == tm and t_lat % tm == 0 and tm % GRID_W == 0
    nb64 = n_rows // GRID_W
    c2 = lambda i: (0, 0)
    c3 = lambda i: (0, 0, 0)
    pm_spec = pl.BlockSpec((C_PAIRS, tm, LANES), lambda i: (0, i, 0))
    pm_dtypes = (BF16, BF16, BF16, BF16, F32, F32, BF16, BF16, BF16, BF16)
    pm_shapes = [jax.ShapeDtypeStruct((C_PAIRS, n_rows, LANES), dt) for dt in pm_dtypes]
    return pl.pallas_call(
        functools.partial(_rw_in_kernel, tm=tm, t_lat=t_lat, t_ctx=t_ctx, n_lat=n_lat),
        grid=(n_rows // tm,),
        in_specs=[pl.BlockSpec((tm, d), lambda i: (i, 0)),
                  pl.BlockSpec((GRID_W, d), lambda i: (jnp.maximum(i * (tm // GRID_W) - 1, 0), 0)),
                  pl.BlockSpec((GRID_W, d), lambda i: (jnp.minimum((i + 1) * (tm // GRID_W), nb64 - 1), 0)),
                  pl.BlockSpec((6, d), c2), pl.BlockSpec((8, d), c2),
                  pl.BlockSpec((d, d), c2), pl.BlockSpec((d, d), c2), pl.BlockSpec((d, d), c2),
                  pl.BlockSpec((d, LANES), c2), pl.BlockSpec((2, 64, d), c3),
                  pl.BlockSpec((d, LANES), c2), pl.BlockSpec((2, 64, d), c3),
                  pl.BlockSpec((d, LANES), c2), pl.BlockSpec((LANES, d), c2),
                  pl.BlockSpec((LANES, LANES), c2)],
        out_specs=[pm_spec] * 10 + [pl.BlockSpec((tm, d), lambda i: (i, 0))],
        out_shape=pm_shapes + [jax.ShapeDtypeStruct((n_rows, d), BF16)],
        compiler_params=_cparams(("parallel",)),
        name="rw_in",
    )(h, h, h, mix, vecs, wr, wk, wv, w1, w2, a1, a2, g1, g2, bd)


def _rwkv_kernel(*refs):
    ins_f, ins_b = refs[0:6], refs[6:12]
    y_f, y_b, st = refs[12:]

    @pl.when(pl.program_id(1) == 0)
    def _():
        st[...] = jnp.zeros_like(st)

    L = C_CHUNK
    ri = _iota((L, L), 0)
    ci = _iota((L, L), 1)
    si = _iota((2 * L, 2 * L), 0) % L
    sj = _iota((2 * L, 2 * L), 1) % L
    incl = (si >= sj, si <= sj)
    strict = (si > sj, si < sj)
    eye = jnp.where(_iota((2 * L, 2 * L), 0) == _iota((2 * L, 2 * L), 1), 1.0, 0.0)
    lo_lane = _iota((L, LANES), 1) < C_HEAD
    dn = ((1,), (0,))

    def stack(x):
        return jnp.concatenate([jnp.where(lo_lane, x, 0.0), jnp.where(lo_lane, 0.0, x)], axis=0)

    refs_d = (ins_f, ins_b)
    y_refs = (y_f, y_b)
    for g0 in range(0, C_PAIRS, RW_GROUP):
        chains = [(d, p) for p in range(g0, g0 + RW_GROUP) for d in range(2)]
        n = len(chains)
        cs = [_cumsum_rows(refs_d[d][1][p], d == 1) for d, p in chains]
        big, vs, xa0r0, xbke, dec = [], [], [], [], []
        for (d, p), c in zip(chains, cs):
            r_ref, lw_ref, k_ref, v_ref, kk_ref, b_ref = refs_d[d]
            r, k, v, b = [x[p].astype(F32) for x in (r_ref, k_ref, v_ref, b_ref)]
            lw = lw_ref[p]
            a = -kk_ref[p].astype(F32)
            cex = c - lw
            if d == 0:
                mid, c_end = c[L // 2 - 1:L // 2, :], c[L - 1:L, :]
            else:
                mid, c_end = c[L // 2:L // 2 + 1, :], c[0:1, :]
            e_k = jnp.exp(mid - c)
            e_end = jnp.exp(c_end - c)
            lhs = jnp.concatenate([stack(a * jnp.exp(cex - mid)), stack(r * jnp.exp(c - mid))], axis=0)
            rhs = jnp.concatenate([stack(b * e_k), stack(k * e_k)], axis=0)
            big.append(_mm_nt(lhs, rhs))
            xa0r0.append(jnp.concatenate([stack(a * jnp.exp(cex)), stack(r * jnp.exp(c))], axis=0).astype(BF16))
            xbke.append(jnp.concatenate([stack(b * e_end), stack(k * e_end)], axis=0).astype(BF16))
            vs.append(stack(v))
            dec.append(jnp.exp(c_end))
        qs = [jnp.where(strict[d], bg[:2 * L, :2 * L], 0.0) for (d, _), bg in zip(chains, big)]
        n_ak = [jnp.where(strict[d], bg[:2 * L, 2 * L:], 0.0).astype(BF16) for (d, _), bg in zip(chains, big)]
        m_r = [jnp.where(jnp.concatenate([incl[d], incl[d]], axis=1), bg[2 * L:, :], 0.0).astype(BF16)
               for (d, _), bg in zip(chains, big)]
        ps = [eye + q for q in qs]
        qb = [q.astype(BF16) for q in qs]
        qs = [_dot(q, q, dn) for q in qb]
        for _ in range(4):
            qb = [q.astype(BF16) for q in qs]
            res = [_dot(q, jnp.concatenate([q, pp.astype(BF16)], axis=1), dn) for q, pp in zip(qb, ps)]
            qs = [r_[:, :2 * L] for r_ in res]
            ps = [pp + r_[:, 2 * L:] for pp, r_ in zip(ps, res)]
        ps = [pp + _mm(q, pp) for pp, q in zip(ps, qs)]
        s_prev = [st[d, p] for d, p in chains]
        sa = [_mm_nt(x, s) for x, s in zip(xa0r0, s_prev)]
        rhs = [sa[i][:2 * L] + _mm(n_ak[i], vs[i]) for i in range(n)]
        us = [_mm(ps[i], rhs[i]) for i in range(n)]
        uv = [jnp.concatenate([us[i], vs[i]], axis=0).astype(BF16) for i in range(n)]
        ys = [sa[i][2 * L:] + _mm(m_r[i], uv[i]) for i in range(n)]
        for i, (d, p) in enumerate(chains):
            st[d, p] = s_prev[i] * dec[i] + _mm_tn(uv[i], xbke[i])
            y_refs[d][p] = (ys[i][:L] + ys[i][L:]).astype(y_refs[d].dtype)


def _rwkv_scan(r, v, kk, lw, k, b, *, n_rows, t_lat, t_ctx, bsz):
    ch = C_CHUNK
    nlat, nctx = t_lat // ch, t_ctx // ch
    fwd = functools.partial(_scan_block, nlat=nlat, nctx=nctx, bsz=bsz, reverse=False)
    bwd = functools.partial(_scan_block, nlat=nlat, nctx=nctx, bsz=bsz, reverse=True)
    sf = pl.BlockSpec((C_PAIRS, ch, LANES), lambda bb, s: (0, fwd(bb, s), 0))
    sb = pl.BlockSpec((C_PAIRS, ch, LANES), lambda bb, s: (0, bwd(bb, s), 0))
    return pl.pallas_call(
        _rwkv_kernel,
        grid=(bsz, nlat + nctx),
        in_specs=[sf] * 6 + [sb] * 6,
        out_specs=[sf, sb],
        out_shape=[jax.ShapeDtypeStruct((C_PAIRS, n_rows, LANES), BF16)] * 2,
        scratch_shapes=[pltpu.VMEM((2, C_PAIRS, LANES, LANES), F32)],
        compiler_params=_cparams(("parallel", "arbitrary")),
        name="rwkv_scan",
    )(r, lw[0], k[0], v, kk, b[0], r, lw[1], k[1], v, kk, b[1])


def _rw_out_kernel(x_ref, m_ref, yf, yb, bv, g_ref, lnw, lnb, bd_ref, w_ref, o_ref):
    bd = bd_ref[...]
    parts = []
    for p in range(C_PAIRS):
        cs = slice(p * LANES, (p + 1) * LANES)
        y = yf[p].astype(F32) + yb[p].astype(F32)
        mu = _mm_exact_r(y, bd) * (1.0 / C_HEAD)
        yc = y - mu
        var = _mm_exact_r(yc * yc, bd) * (1.0 / C_HEAD)
        yn = yc * lax.rsqrt(var + LNX_EPS) * lnw[:, cs] + lnb[:, cs] + bv[p].astype(F32)
        parts.append((yn * g_ref[:, cs].astype(F32)).astype(BF16))
    cat = jnp.concatenate(parts, axis=-1)
    o_ref[...] = x_ref[...] + m_ref[5:6, :] * _dot(cat, w_ref[...], ((1,), (0,)))


def _rw_out(x, mod, yf, yb, bv, g, lnw, lnb, bd, w, *, n_rows, t_lat, bsz):
    d = x.shape[1]
    tm = 256
    row = lambda i: (i, 0)
    const = lambda i: (0, 0)
    pm = pl.BlockSpec((C_PAIRS, tm, LANES), lambda i: (0, i, 0))
    return pl.pallas_call(
        _rw_out_kernel,
        grid=(n_rows // tm,),
        in_specs=[pl.BlockSpec((tm, d), row),
                  pl.BlockSpec((None, N_ADA, d), lambda i: (_mod_row(i, tm, t_lat, bsz), 0, 0)),
                  pm, pm, pm, pl.BlockSpec((tm, d), row),
                  pl.BlockSpec((1, d), const), pl.BlockSpec((1, d), const),
                  pl.BlockSpec((LANES, LANES), const), pl.BlockSpec((d, d), const)],
        out_specs=pl.BlockSpec((tm, d), row),
        out_shape=jax.ShapeDtypeStruct((n_rows, d), F32),
        compiler_params=_cparams(("parallel",)),
        name="rw_out",
    )(x, mod, yf, yb, bv, g, lnw.reshape(1, d), lnb.reshape(1, d), bd, w)


def _even_layer(x, mod, g, wi, wo, layer, in_w, out_w, lb, hgrn_g, conv_w, conv_b, dt_bias, a_log, d_skip, ssd_g, dims):
    n_all, t_lat, t_ctx, bsz = dims["n_all"], dims["t_lat"], dims["t_ctx"], dims["bsz"]
    n_lat = bsz * t_lat
    x, h = _ffn_half(x, mod, g[0], wi, wo, (layer, 0), slot=0, n_rows=n_all, t_lat=t_lat, bsz=bsz,
                     post="prenorm", g2=g[1], h_dtype=BF16)
    w_pad = jnp.pad(in_w, ((0, 0), (0, EVEN_IN_PAD - EVEN_IN))).astype(BF16)
    dtb = jnp.pad(dt_bias.reshape(1, 2 * B_HEADS), ((0, 0), (0, LANES - 2 * B_HEADS)))
    qi, f, gz, xbc, dt = _even_in(h, w_pad, lb.reshape(1, 2 * A_DIM), dtb, conv_w, conv_b,
                                  n_rows=n_all, t_lat=t_lat, t_ctx=t_ctx, n_lat=n_lat)
    oaf, oab = _hgrn_scan(qi, f, n_rows=n_all, t_lat=t_lat, t_ctx=t_ctx, bsz=bsz)
    a_row = jnp.pad(-jnp.exp(a_log.astype(F32)).reshape(1, 2 * B_HEADS), ((0, 0), (0, LANES - 2 * B_HEADS)))
    dsk_row = jnp.repeat(d_skip.astype(F32), B_HEADDIM).reshape(1, B_DIM)
    e_np = np.zeros((2, LANES, B_DIM), np.float32)
    for dd in range(2):
        for hh in range(B_HEADS):
            e_np[dd, dd * B_HEADS + hh, hh * B_HEADDIM:(hh + 1) * B_HEADDIM] = 1.0
    ybf, ybb = _ssd_scan(xbc, dt, a_row, dsk_row, jnp.asarray(e_np, BF16),
                         n_rows=n_all, t_lat=t_lat, t_ctx=t_ctx, bsz=bsz)
    x = _even_out(x, mod, oaf, oab, ybf, ybb, gz, hgrn_g.reshape(1, A_DK), ssd_g.reshape(1, B_DIM),
                  out_w.astype(BF16), n_rows=n_all, t_lat=t_lat, bsz=bsz)
    return x


def _odd_layer(x, mod, g, wi, wo, layer, p, dims):
    n_all, t_lat, t_ctx, bsz = dims["n_all"], dims["t_lat"], dims["t_ctx"], dims["bsz"]
    n_lat = bsz * t_lat
    d = D_MODEL
    x, h = _ffn_half(x, mod, g[0], wi, wo, (layer, 0), slot=0, n_rows=n_all, t_lat=t_lat, bsz=bsz,
                     post="prenorm", g2=g[1], h_dtype=F32)
    vecs = jnp.concatenate([p["w0"], p["a0"], p["k_k"][None], p["k_a"][None], p["r_k"].reshape(1, d),
                            jnp.zeros((1, d), F32)], axis=0)
    w1 = jnp.concatenate([p["w1"][0], p["w1"][1]], axis=1).astype(BF16)
    a1 = jnp.concatenate([p["a1"][0], p["a1"][1]], axis=1).astype(BF16)
    bd_np = np.kron(np.eye(2, dtype=np.float32), np.ones((C_HEAD, C_HEAD), np.float32))
    bd = jnp.asarray(bd_np, BF16)
    outs = _rw_in(h, p["x_mix"], vecs, p["wr"].astype(BF16), p["wk"].astype(BF16), p["wv"].astype(BF16),
                  w1, p["w2"].astype(BF16), a1, p["a2"].astype(BF16), p["g1"].astype(BF16),
                  p["g2"].astype(BF16), bd, n_rows=n_all, t_lat=t_lat, t_ctx=t_ctx, n_lat=n_lat)
    r, v, kk, bv, lw0, lw1, k0, k1, b0, b1, gg = outs
    yf, yb = _rwkv_scan(r, v, kk, (lw0, lw1), (k0, k1), (b0, b1), n_rows=n_all, t_lat=t_lat, t_ctx=t_ctx, bsz=bsz)
    x = _rw_out(x, mod, yf, yb, bv, gg, p["ln_w"], p["ln_b"], bd, p["wo"].astype(BF16),
                n_rows=n_all, t_lat=t_lat, bsz=bsz)
    return x


def kernel(x, c, ctx, c_ctx, ada_w, ada_b, norm_g, ffn_wi, ffn_wo, final_g, hgrn_lb_logits, ev_in_w, ev_out_w, hgrn_norm_g, ssd_conv_w, ssd_conv_b, ssd_dt_bias, ssd_a_log, ssd_d, ssd_norm_g, rw_x_mix, rw_wr, rw_wk, rw_wv, rw_wo, rw_w0, rw_w1, rw_w2, rw_a0, rw_a1, rw_a2, rw_g1, rw_g2, rw_k_k, rw_k_a, rw_r_k, rw_ln_w, rw_ln_b):
    bsz, t_lat, d = x.shape
    t_ctx = ctx.shape[1]
    depth = ada_w.shape[0]
    n_lat = bsz * t_lat
    n_all = n_lat + bsz * t_ctx
    dims = dict(n_all=n_all, t_lat=t_lat, t_ctx=t_ctx, bsz=bsz)
    assert bsz + 1 <= 8

    xs = (x.reshape(n_lat, d), ctx.reshape(bsz * t_ctx, d))
    cvec = jnp.concatenate([c, c_ctx[None], jnp.zeros((8 - bsz - 1, d), F32)], axis=0)
    mod_all = _modulation(cvec, ada_w, ada_b).reshape(depth, 8, N_ADA, d)
    lb_all = jnp.cumsum(jax.nn.softmax(hgrn_lb_logits.astype(F32), axis=0), axis=0)
    wi_bf = ffn_wi.astype(BF16)
    wo_bf = ffn_wo.astype(BF16)

    for layer in range(depth):
        last = layer == depth - 1
        mod = mod_all[layer]
        g = norm_g[layer]
        if layer % 2 == 0:
            e = layer // 2
            xs = _even_layer(xs, mod, g, wi_bf, wo_bf, layer, ev_in_w[e], ev_out_w[e], lb_all[layer],
                             hgrn_norm_g[e], ssd_conv_w[e], ssd_conv_b[e], ssd_dt_bias[e], ssd_a_log[e],
                             ssd_d[e], ssd_norm_g[e], dims)
        else:
            o = layer // 2
            p = dict(x_mix=rw_x_mix[o], wr=rw_wr[o], wk=rw_wk[o], wv=rw_wv[o], wo=rw_wo[o], w0=rw_w0[o],
                     w1=rw_w1[o], w2=rw_w2[o], a0=rw_a0[o], a1=rw_a1[o], a2=rw_a2[o], g1=rw_g1[o], g2=rw_g2[o],
                     k_k=rw_k_k[o], k_a=rw_k_a[o], r_k=rw_r_k[o], ln_w=rw_ln_w[o], ln_b=rw_ln_b[o])
            xs = _odd_layer(xs, mod, g, wi_bf, wo_bf, layer, p, dims)
        if last:
            out = _ffn_half(xs, mod, g[2], wi_bf, wo_bf, (layer, 1), slot=2, n_rows=n_lat, t_lat=t_lat,
                            bsz=bsz, post="final", g2=final_g)
        else:
            xs = _ffn_half(xs, mod, g[2], wi_bf, wo_bf, (layer, 1), slot=2, n_rows=n_all, t_lat=t_lat,
                           bsz=bsz)
    return out.reshape(bsz, t_lat, d)
```

```python
import functools

import numpy as np
import jax
import jax.numpy as jnp
from jax import lax
from jax.experimental import pallas as pl
from jax.experimental.pallas import tpu as pltpu

F32 = jnp.float32
BF16 = jnp.bfloat16

D_MODEL = 1024
N_ADA = 9
FF_DIM = 2816
EPS = 1e-6
GRID_W = 64
A_HEADS = 4
A_DK = 128
A_DIM = A_HEADS * A_DK
A_CHUNK = 32
B_HEADS = 8
B_HEADDIM = 64
B_DIM = B_HEADS * B_HEADDIM
B_GROUPS = 2
B_STATE = 128
B_CHUNK = 128
B_XBC = B_DIM + 2 * B_GROUPS * B_STATE
EVEN_IN = A_DIM * 5 + B_DIM + B_XBC + 2 * B_HEADS
EVEN_IN_PAD = 4224
C_HEAD = 64
C_HEADS = D_MODEL // C_HEAD
C_PAIRS = C_HEADS // 2
C_CHUNK = 64
LNX_EPS = 64e-5
RW_GROUP = 4
LANES = 128

VMEM_LIMIT = 56 * 1024 * 1024


def _cparams(sem):
    return pltpu.CompilerParams(dimension_semantics=sem, vmem_limit_bytes=VMEM_LIMIT)


def _sigmoid(x):
    return 1.0 / (1.0 + jnp.exp(-x))


def _silu(x):
    return x * _sigmoid(x)


def _softplus(x):
    return jnp.maximum(x, 0.0) + jnp.log1p(jnp.exp(-jnp.abs(x)))


def _dot(a, b, dims):
    return lax.dot_general(a, b, (dims, ((), ())), preferred_element_type=F32)


def _mm(a, b):
    return _dot(a.astype(BF16), b.astype(BF16), ((1,), (0,)))


def _mm_nt(a, b):
    return _dot(a.astype(BF16), b.astype(BF16), ((1,), (1,)))


def _mm_tn(a, b):
    return _dot(a.astype(BF16), b.astype(BF16), ((0,), (0,)))


def _split2(x):
    hi = x.astype(BF16)
    lo = (x - hi.astype(F32)).astype(BF16)
    return hi, lo


def _split3(x):
    x1 = x.astype(BF16)
    r1 = x - x1.astype(F32)
    x2 = r1.astype(BF16)
    x3 = (r1 - x2.astype(F32)).astype(BF16)
    return x1, x2, x3


def _cumsum_rows(x, reverse):
    n = x.shape[0]
    row = _iota(x.shape, 0)
    s = 1
    while s < n:
        if reverse:
            x = x + jnp.where(row < n - s, pltpu.roll(x, n - s, 0), 0.0)
        else:
            x = x + jnp.where(row >= s, pltpu.roll(x, s, 0), 0.0)
        s *= 2
    return x


def _mm_exact_r(x, m_bf):
    x1, x2, x3 = _split3(x)
    dn = ((1,), (0,))
    return _dot(x1, m_bf, dn) + _dot(x2, m_bf, dn) + _dot(x3, m_bf, dn)


def _mm_hp(a, b):
    ah, al = _split2(a)
    bh, bl = _split2(b)
    dn = ((1,), (0,))
    return _dot(ah, bh, dn) + _dot(ah, bl, dn) + _dot(al, bh, dn)


def _rms(x):
    return x * lax.rsqrt(jnp.mean(x * x, axis=-1, keepdims=True) + EPS)


def _iota(shape, dim):
    return lax.broadcasted_iota(jnp.int32, shape, dim)


def _mod_row(i, tm, t_lat, bsz):
    return jnp.minimum((i * tm) // t_lat, bsz)


def _seq_pos(r0, t_lat, t_ctx, n_lat):
    in_lat = r0 < n_lat
    pos = jnp.where(in_lat, r0 % t_lat, (r0 - n_lat) % t_ctx)
    slen = jnp.where(in_lat, t_lat, t_ctx)
    return in_lat, pos, slen


def _scan_block(b, s, nlat, nctx, bsz, reverse):
    if reverse:
        cblk = bsz * nlat + b * nctx + (nctx - 1 - s)
        lblk = b * nlat + (nlat - 1 - (s - nctx))
    else:
        cblk = bsz * nlat + b * nctx + s
        lblk = b * nlat + (s - nctx)
    return jnp.where(s < nctx, cblk, lblk)


def _mod_kernel(c_ref, w_ref, b_ref, o_ref):
    sc = _silu(c_ref[...])
    o_ref[...] = _mm(sc, w_ref[...]) + b_ref[...]


def _modulation(cvec, ada_w, ada_b):
    depth, d, n = ada_w.shape
    tn = 1536
    return pl.pallas_call(
        _mod_kernel,
        grid=(depth, n // tn),
        in_specs=[pl.BlockSpec((8, d), lambda l, j: (0, 0)),
                  pl.BlockSpec((None, d, tn), lambda l, j: (l, 0, j)),
                  pl.BlockSpec((None, 1, tn), lambda l, j: (l, 0, j))],
        out_specs=pl.BlockSpec((None, 8, tn), lambda l, j: (l, 0, j)),
        out_shape=jax.ShapeDtypeStruct((depth, 8, n), F32),
        compiler_params=_cparams(("arbitrary", "arbitrary")),
        name="ada_mod",
    )(cvec, ada_w, ada_b.reshape(depth, 1, n))


def _ffn_kernel(*refs, slot, tf, post, n_lat_tiles):
    if n_lat_tiles is None:
        x_ref = refs[0]
        x = x_ref[...]
    else:
        x = jnp.where(pl.program_id(0) >= n_lat_tiles, refs[1][...], refs[0][...])
        refs = refs[1:]
    m_ref, g_ref, wi_ref, wo_ref = refs[1:5]
    if post == "prenorm":
        g2_ref, o_ref, h_ref, acc = refs[5:]
    elif post == "final":
        g2_ref, o_ref, acc = refs[5:]
    else:
        o_ref, acc = refs[5:]
    dn = ((1,), (0,))
    nch = FF_DIM // tf
    h = _rms(x) * g_ref[...] * (1.0 + m_ref[3 * slot + 1:3 * slot + 2, :]) + m_ref[3 * slot:3 * slot + 1, :]
    h = h.astype(BF16)

    def gate_up(c):
        return (_dot(h, wi_ref[:, c * tf:(c + 1) * tf], dn),
                _dot(h, wi_ref[:, FF_DIM + c * tf:FF_DIM + (c + 1) * tf], dn))

    nxt = gate_up(0)
    for c in range(nch):
        gt, up = nxt
        if c + 1 < nch:
            nxt = gate_up(c + 1)
        part = _dot((_silu(gt) * up).astype(BF16), wo_ref[c * tf:(c + 1) * tf, :], dn)
        if c == 0:
            acc[...] = part
        else:
            acc[...] += part

    xn = x + 0.5 * m_ref[3 * slot + 2:3 * slot + 3, :] * acc[...]
    if post == "final":
        o_ref[...] = _rms(xn) * g2_ref[...]
    else:
        o_ref[...] = xn
    if post == "prenorm":
        h2 = _rms(xn) * g2_ref[...] * (1.0 + m_ref[4:5, :]) + m_ref[3:4, :]
        h_ref[...] = h2.astype(h_ref.dtype)


def _ffn_half(x, mod, g, wi_all, wo_all, sel, *, slot, n_rows, t_lat, bsz, post=None, g2=None, h_dtype=F32):
    split = isinstance(x, tuple)
    d = D_MODEL
    tm = 512
    while n_rows % tm or t_lat % tm:
        tm //= 2
    const = lambda i: (0, 0)
    resident = pl.Buffered(1)
    if split:
        n_lat_tiles = x[0].shape[0] // tm
        x_specs = [pl.BlockSpec((tm, d), lambda i: (jnp.minimum(i, n_lat_tiles - 1), 0)),
                   pl.BlockSpec((tm, d), lambda i: (jnp.maximum(i - n_lat_tiles, 0), 0))]
        x_args = list(x)
    else:
        n_lat_tiles = None
        x_specs = [pl.BlockSpec((tm, d), lambda i: (i, 0))]
        x_args = [x]
    in_specs = x_specs + [
        pl.BlockSpec((None, N_ADA, d), lambda i: (_mod_row(i, tm, t_lat, bsz), 0, 0)),
        pl.BlockSpec((1, d), const),
        pl.BlockSpec((None, None, d, 2 * FF_DIM), lambda i: sel + (0, 0), pipeline_mode=resident),
        pl.BlockSpec((None, None, FF_DIM, d), lambda i: sel + (0, 0), pipeline_mode=resident)]
    args = x_args + [mod, g.reshape(1, d), wi_all, wo_all]
    out_spec = pl.BlockSpec((tm, d), lambda i: (i, 0))
    out_shape = jax.ShapeDtypeStruct((n_rows, d), F32)
    if post is not None:
        in_specs.append(pl.BlockSpec((1, d), const))
        args.append(g2.reshape(1, d))
    if post == "prenorm":
        out_specs = [out_spec, out_spec]
        out_shapes = [out_shape, jax.ShapeDtypeStruct((n_rows, d), h_dtype)]
    else:
        out_specs, out_shapes = out_spec, out_shape
    return pl.pallas_call(
        functools.partial(_ffn_kernel, slot=slot, tf=256, post=post, n_lat_tiles=n_lat_tiles),
        grid=(n_rows // tm,),
        in_specs=in_specs,
        out_specs=out_specs,
        out_shape=out_shapes,
        scratch_shapes=[pltpu.VMEM((tm, d), F32)],
        compiler_params=_cparams(("parallel",)),
        name="ffn_half",
    )(*args)


HALO = 16


def _even_in_kernel(h_ref, hp_ref, hn_ref, w_ref, lb_ref, dtb_ref, cw_ref, cb_ref,
                    qi_ref, f_ref, gz_ref, xbc_ref, dt_ref, *, tm, t_lat, t_ctx, n_lat):
    h = h_ref[...]
    dn = ((1,), (0,))
    qi = _dot(h, w_ref[:, 0:1024], dn)
    qi_ref[:, 0:512] = _silu(qi[:, 0:512]).astype(qi_ref.dtype)
    qi_ref[:, 512:1024] = qi[:, 512:1024].astype(qi_ref.dtype)
    lb = lb_ref[...]
    f_ref[...] = lb + (1.0 - lb) * _sigmoid(_dot(h, w_ref[:, 1024:2048], dn))
    gz_ref[...] = _silu(_dot(h, w_ref[:, 2048:3072], dn)).astype(gz_ref.dtype)
    dt_ref[...] = _softplus(_dot(h, w_ref[:, 4096:4224], dn) + dtb_ref[...])
    n_ext = tm + 2 * HALO
    xe = _dot(jnp.concatenate([hp_ref[...], h, hn_ref[...]], axis=0), w_ref[:, 3072:4096], dn)
    x = xe[HALO:HALO + tm]
    xp = pltpu.roll(xe, 1, 0)[HALO:HALO + tm]
    xn = pltpu.roll(xe, n_ext - 1, 0)[HALO:HALO + tm]
    r = pl.program_id(0) * tm + _iota((tm, 1), 0)
    in_lat = r < n_lat
    pos = jnp.where(in_lat, r % t_lat, (r - n_lat) % t_ctx)
    slen = jnp.where(in_lat, t_lat, t_ctx)
    xp = jnp.where(pos == 0, 0.0, xp)
    xn = jnp.where(pos == slen - 1, 0.0, xn)
    y = cw_ref[0:1, :] * xp + cw_ref[1:2, :] * x + cw_ref[2:3, :] * xn + cb_ref[...]
    xbc_ref[...] = _silu(y).astype(xbc_ref.dtype)


def _even_in(h, w, lb, dtb, conv_w, conv_b, *, n_rows, t_lat, t_ctx, n_lat):
    d = h.shape[1]
    tm = 512
    while n_rows % tm:
        tm //= 2
    nbh = n_rows // HALO
    widths = (1024, 1024, 1024, B_XBC, LANES)
    dtypes = (BF16, F32, BF16, BF16, F32)
    const = lambda i: (0, 0)
    return pl.pallas_call(
        functools.partial(_even_in_kernel, tm=tm, t_lat=t_lat, t_ctx=t_ctx, n_lat=n_lat),
        grid=(n_rows // tm,),
        in_specs=[pl.BlockSpec((tm, d), lambda i: (i, 0)),
                  pl.BlockSpec((HALO, d), lambda i: (jnp.maximum(i * (tm // HALO) - 1, 0), 0)),
                  pl.BlockSpec((HALO, d), lambda i: (jnp.minimum((i + 1) * (tm // HALO), nbh - 1), 0)),
                  pl.BlockSpec((d, EVEN_IN_PAD), const, pipeline_mode=pl.Buffered(1)),
                  pl.BlockSpec((1, 1024), const),
                  pl.BlockSpec((1, LANES), const),
                  pl.BlockSpec((3, B_XBC), const),
                  pl.BlockSpec((1, B_XBC), const)],
        out_specs=[pl.BlockSpec((tm, wd), lambda i: (i, 0)) for wd in widths],
        out_shape=[jax.ShapeDtypeStruct((n_rows, wd), dt) for wd, dt in zip(widths, dtypes)],
        compiler_params=_cparams(("parallel",)),
        name="even_in",
    )(h, h, h, w, lb, dtb, conv_w, conv_b.reshape(1, B_XBC))


def _hgrn_kernel(qi_f, f_f, qi_b, f_b, o_f, o_b, st, *, ch):
    @pl.when(pl.program_id(1) == 0)
    def _():
        st[...] = jnp.zeros_like(st)

    L = A_CHUNK
    nsub = ch // L
    ri = _iota((L, L), 0)
    ci = _iota((L, L), 1)
    incl = (ri >= ci, ri <= ci)
    qis, fs, outs = (qi_f, qi_b), (f_f, f_b), (o_f, o_b)

    chains = [(d, h) for h in range(A_HEADS) for d in range(2)]

    def rows_of(j, d):
        jj = j if d == 0 else nsub - 1 - j
        return slice(jj * L, (jj + 1) * L)

    def local_stage(j):
        out = []
        for d, h in chains:
            cs = slice(h * A_DK, (h + 1) * A_DK)
            rows = rows_of(j, d)
            q = qis[d][rows, cs].astype(F32)
            v = qis[d][rows, A_DIM + h * A_DK:A_DIM + (h + 1) * A_DK].astype(BF16)
            f = fs[d][rows, cs]
            k = 1.0 - f
            bc = _cumsum_rows(jnp.log(f), d == 1)
            if d == 0:
                blast, bref = bc[L - 1:L, :], bc[L // 2 - 1:L // 2, :]
            else:
                blast, bref = bc[0:1, :], bc[L // 2:L // 2 + 1, :]
            out.append(dict(
                v=v, dec=jnp.exp(blast), qe=(q * jnp.exp(bc)).astype(BF16),
                ds=_mm_tn(v, k * jnp.exp(blast - bc)),
                sc=_mm_nt(q * jnp.exp(bc - bref), k * jnp.exp(bref - bc))))
        return out

    states = [st[d, h] for d, h in chains]
    nxt = local_stage(0)
    for j in range(nsub):
        cur = nxt
        if j + 1 < nsub:
            nxt = local_stage(j + 1)
        o2s = [_mm_nt(c["qe"], s) for c, s in zip(cur, states)]
        states = [c["dec"] * s + c["ds"] for c, s in zip(cur, states)]
        o1s = [_mm(jnp.where(incl[d], c["sc"], 0.0), c["v"]) for (d, _), c in zip(chains, cur)]
        for i, (d, h) in enumerate(chains):
            outs[d][rows_of(j, d), h * A_DK:(h + 1) * A_DK] = (o1s[i] + o2s[i]).astype(outs[d].dtype)
    for i, (d, h) in enumerate(chains):
        st[d, h] = states[i]


def _hgrn_scan(qi, f, *, n_rows, t_lat, t_ctx, bsz):
    ch = 256
    nlat, nctx = t_lat // ch, t_ctx // ch
    fwd = functools.partial(_scan_block, nlat=nlat, nctx=nctx, bsz=bsz, reverse=False)
    bwd = functools.partial(_scan_block, nlat=nlat, nctx=nctx, bsz=bsz, reverse=True)
    return pl.pallas_call(
        functools.partial(_hgrn_kernel, ch=ch),
        grid=(bsz, nlat + nctx),
        in_specs=[pl.BlockSpec((ch, 2 * A_DIM), lambda b, s: (fwd(b, s), 0)),
                  pl.BlockSpec((ch, A_DIM), lambda b, s: (fwd(b, s), 0)),
                  pl.BlockSpec((ch, 2 * A_DIM), lambda b, s: (bwd(b, s), 0)),
                  pl.BlockSpec((ch, A_DIM), lambda b, s: (bwd(b, s), 1))],
        out_specs=[pl.BlockSpec((ch, A_DIM), lambda b, s: (fwd(b, s), 0)),
                   pl.BlockSpec((ch, A_DIM), lambda b, s: (bwd(b, s), 0))],
        out_shape=[jax.ShapeDtypeStruct((n_rows, A_DIM), BF16)] * 2,
        scratch_shapes=[pltpu.VMEM((2, A_HEADS, A_DK, A_DK), F32)],
        compiler_params=_cparams(("parallel", "arbitrary")),
        name="hgrn_scan",
    )(qi, f, qi, f)


def _ssd_kernel(x_f, dt_f, x_b, dt_b, a_ref, dsk_ref, e_ref, o_f, o_b, st):
    @pl.when(pl.program_id(1) == 0)
    def _():
        st[...] = jnp.zeros_like(st)

    L = B_CHUNK
    hg = B_HEADS // B_GROUPS
    gw = hg * B_HEADDIM
    ri = _iota((L, L), 0)
    ci = _iota((L, L), 1)
    incl = (ri >= ci, ri <= ci)
    xs, dts, outs = (x_f, x_b), (dt_f, dt_b), (o_f, o_b)
    dt = [dts[d][...] for d in range(2)]
    acum = [_cumsum_rows(dt[d] * a_ref[...], d == 1) for d in range(2)]
    acum_e = [_mm_exact_r(acum[d], e_ref[d]) for d in range(2)]
    dt_e = [_mm_exact_r(dt[d], e_ref[d]) for d in range(2)]
    alast_e = [acum_e[0][L - 1:L, :], acum_e[1][0:1, :]]
    acum_t = [acum[d].T for d in range(2)]
    xa = [xs[d][:, 0:B_DIM].astype(F32) for d in range(2)]
    xdt = [xa[d] * dt_e[d] for d in range(2)]
    xdec = [xdt[d] * jnp.exp(alast_e[d] - acum_e[d]) for d in range(2)]
    dg = [(d, g) for d in range(2) for g in range(B_GROUPS)]
    bgs = [xs[d][:, B_DIM + g * B_STATE:B_DIM + (g + 1) * B_STATE].astype(BF16) for d, g in dg]
    cgs = [xs[d][:, B_DIM + (B_GROUPS + g) * B_STATE:B_DIM + (B_GROUPS + g + 1) * B_STATE].astype(BF16)
           for d, g in dg]
    cbs = [_mm_nt(cgs[i], bgs[i]) for i in range(len(dg))]
    s_prev = [st[d, g] for d, g in dg]
    y_off = [_mm(cgs[i], s_prev[i]) * jnp.exp(acum_e[d][:, g * gw:(g + 1) * gw]) for i, (d, g) in enumerate(dg)]
    for i, (d, g) in enumerate(dg):
        gs = slice(g * gw, (g + 1) * gw)
        st[d, g] = s_prev[i] * jnp.exp(alast_e[d][:, gs]) + _mm_tn(bgs[i], xdec[d][:, gs])
    ys = []
    for i, (d, g) in enumerate(dg):
        for hh in range(hg):
            h = g * hg + hh
            j = d * B_HEADS + h
            seg = acum[d][:, j:j + 1] - acum_t[d][j:j + 1, :]
            lm = jnp.exp(jnp.where(incl[d], seg, -1e30))
            ys.append(_mm(cbs[i] * lm, xdt[d][:, h * B_HEADDIM:(h + 1) * B_HEADDIM]))
    for i, (d, g) in enumerate(dg):
        for hh in range(hg):
            h = g * hg + hh
            hs = slice(h * B_HEADDIM, (h + 1) * B_HEADDIM)
            y = ys[i * hg + hh] + y_off[i][:, hh * B_HEADDIM:(hh + 1) * B_HEADDIM]
            if d == 0:
                y = y + dsk_ref[:, hs] * xa[d][:, hs]
            outs[d][:, hs] = y.astype(outs[d].dtype)


def _ssd_scan(xbc, dt, a_row, dsk_row, e_mat, *, n_rows, t_lat, t_ctx, bsz):
    ch = B_CHUNK
    nlat, nctx = t_lat // ch, t_ctx // ch
    fwd = functools.partial(_scan_block, nlat=nlat, nctx=nctx, bsz=bsz, reverse=False)
    bwd = functools.partial(_scan_block, nlat=nlat, nctx=nctx, bsz=bsz, reverse=True)
    const2 = lambda b, s: (0, 0)
    return pl.pallas_call(
        _ssd_kernel,
        grid=(bsz, nlat + nctx),
        in_specs=[pl.BlockSpec((ch, B_XBC), lambda b, s: (fwd(b, s), 0)),
                  pl.BlockSpec((ch, LANES), lambda b, s: (fwd(b, s), 0)),
                  pl.BlockSpec((ch, B_XBC), lambda b, s: (bwd(b, s), 0)),
                  pl.BlockSpec((ch, LANES), lambda b, s: (bwd(b, s), 0)),
                  pl.BlockSpec((1, LANES), const2),
                  pl.BlockSpec((1, B_DIM), const2),
                  pl.BlockSpec((2, LANES, B_DIM), lambda b, s: (0, 0, 0))],
        out_specs=[pl.BlockSpec((ch, B_DIM), lambda b, s: (fwd(b, s), 0)),
                   pl.BlockSpec((ch, B_DIM), lambda b, s: (bwd(b, s), 0))],
        out_shape=[jax.ShapeDtypeStruct((n_rows, B_DIM), BF16)] * 2,
        scratch_shapes=[pltpu.VMEM((2, B_GROUPS, B_STATE, (B_HEADS // B_GROUPS) * B_HEADDIM), F32)],
        compiler_params=_cparams(("parallel", "arbitrary")),
        name="ssd_scan",
    )(xbc, dt, xbc, dt, a_row, dsk_row, e_mat)


def _even_out_kernel(x_ref, m_ref, oaf, oab, ybf, ybb, gz_ref, ga_ref, gb_ref, w_ref, o_ref):
    oa = oaf[...].astype(F32) + oab[...].astype(F32)
    gz = gz_ref[...].astype(F32)
    parts = []
    for h in range(A_HEADS):
        cs = slice(h * A_DK, (h + 1) * A_DK)
        parts.append(_rms(oa[:, cs]) * ga_ref[...] * gz[:, cs])
    ob = (ybf[...].astype(F32) + ybb[...].astype(F32)) * gz[:, A_DIM:A_DIM + B_DIM]
    gw = B_DIM // B_GROUPS
    for g in range(B_GROUPS):
        cs = slice(g * gw, (g + 1) * gw)
        parts.append(_rms(ob[:, cs]) * gb_ref[:, cs])
    cat = jnp.concatenate(parts, axis=-1).astype(BF16)
    y = _dot(cat, w_ref[...], ((1,), (0,)))
    o_ref[...] = x_ref[...] + m_ref[5:6, :] * y


def _even_out(x, mod, oaf, oab, ybf, ybb, gz, ga, gb, w, *, n_rows, t_lat, bsz):
    d = x.shape[1]
    tm = 512
    while n_rows % tm or t_lat % tm:
        tm //= 2
    row = lambda i: (i, 0)
    const = lambda i: (0, 0)
    return pl.pallas_call(
        _even_out_kernel,
        grid=(n_rows // tm,),
        in_specs=[pl.BlockSpec((tm, d), row),
                  pl.BlockSpec((None, N_ADA, d), lambda i: (_mod_row(i, tm, t_lat, bsz), 0, 0)),
                  pl.BlockSpec((tm, A_DIM), row), pl.BlockSpec((tm, A_DIM), row),
                  pl.BlockSpec((tm, B_DIM), row), pl.BlockSpec((tm, B_DIM), row),
                  pl.BlockSpec((tm, A_DIM + B_DIM), row),
                  pl.BlockSpec((1, A_DK), const), pl.BlockSpec((1, B_DIM), const),
                  pl.BlockSpec((A_DIM + B_DIM, d), const)],
        out_specs=pl.BlockSpec((tm, d), row),
        out_shape=jax.ShapeDtypeStruct((n_rows, d), F32),
        compiler_params=_cparams(("parallel",)),
        name="even_out",
    )(x, mod, oaf, oab, ybf, ybb, gz, ga, gb, w)


def _rw_in_kernel(h_ref, up_ref, dn_ref, mix_ref, vec_ref, wr, wk, wv, w1, w2, a1, a2, g1, g2, bd_ref,
                  r_o, v_o, kk_o, bv_o, lw0_o, lw1_o, k0_o, k1_o, b0_o, b1_o, g_o,
                  *, tm, t_lat, t_ctx, n_lat):
    d = D_MODEL
    r0 = pl.program_id(0) * tm
    in_lat, pos, slen = _seq_pos(r0, t_lat, t_ctx, n_lat)
    first = pos == 0
    last = pos + tm == slen
    row = _iota((tm, 1), 0)
    col = row % GRID_W
    ok_prev = jnp.where(in_lat, col, row) != 0
    ok_next = jnp.where(in_lat, col - (GRID_W - 1), row - (tm - 1)) != 0
    q4 = d // 4

    def prev_of(x):
        return jnp.where(ok_prev, pltpu.roll(x, 1, 0), 0.0)

    def next_of(x):
        return jnp.where(ok_next, pltpu.roll(x, tm - 1, 0), 0.0)

    h = h_ref[...]
    h2, h3 = h[:, 2 * q4:3 * q4], h[:, 3 * q4:]
    up = jnp.concatenate([jnp.where(first, 0.0, up_ref[:, 2 * q4:3 * q4]), h2[:tm - GRID_W]], axis=0)
    down = jnp.concatenate([h3[GRID_W:], jnp.where(last, 0.0, dn_ref[:, 3 * q4:])], axis=0)
    hs = jnp.concatenate([prev_of(h[:, :q4]), next_of(h[:, q4:2 * q4]),
                          jnp.where(in_lat, up, prev_of(h2)),
                          jnp.where(in_lat, down, next_of(h3))], axis=1)
    xx = hs - h
    xr, xw, xk, xv, xa, xg = [(h + xx * mix_ref[j:j + 1, :]).astype(BF16) for j in range(6)]
    dn = ((1,), (0,))
    r = _dot(xr, wr[...], dn)
    k = _dot(xk, wk[...], dn)
    v = _dot(xv, wv[...], dn)
    g = _mm(_sigmoid(_dot(xg, g1[...], dn)), g2[...])
    tw = jnp.tanh(_dot(xw, w1[...], dn))
    ta = _dot(xa, a1[...], dn)
    w0, a0 = vec_ref[0:2, :], vec_ref[2:4, :]
    k_k, k_a, r_k = vec_ref[4:5, :], vec_ref[5:6, :], vec_ref[6:7, :]
    bd = bd_ref[...]

    lws, ks, bs, kks = [], [], [], []
    kkr = k * k_k
    for p in range(C_PAIRS):
        cs = slice(p * LANES, (p + 1) * LANES)
        ss = _mm_exact_r(kkr[:, cs] * kkr[:, cs], bd)
        kks.append(kkr[:, cs] * lax.rsqrt(jnp.maximum(ss, 1e-24)))
    for dd in range(2):
        ls = slice(dd * 64, (dd + 1) * 64)
        zw = w0[dd:dd + 1, :] + _mm(tw[:, ls], w2[dd])
        lws.append(-np.float32(np.exp(-0.5)) * _sigmoid(zw))
        asig = _sigmoid(a0[dd:dd + 1, :] + _mm(ta[:, ls], a2[dd]))
        ks.append(k * (1.0 + (asig - 1.0) * k_a))
        bs.append(asig)
    rk = r * (ks[0] + ks[1]) * r_k
    g_o[...] = g.astype(g_o.dtype)
    for p in range(C_PAIRS):
        cs = slice(p * LANES, (p + 1) * LANES)
        r_o[p] = r[:, cs].astype(r_o.dtype)
        v_o[p] = v[:, cs].astype(v_o.dtype)
        kk_o[p] = kks[p].astype(kk_o.dtype)
        bv_o[p] = (_mm_exact_r(rk[:, cs], bd) * v[:, cs]).astype(bv_o.dtype)
        lw0_o[p] = lws[0][:, cs]
        lw1_o[p] = lws[1][:, cs]
        k0_o[p] = ks[0][:, cs].astype(k0_o.dtype)
        k1_o[p] = ks[1][:, cs].astype(k1_o.dtype)
        b0_o[p] = (kks[p] * bs[0][:, cs]).astype(b0_o.dtype)
        b1_o[p] = (kks[p] * bs[1][:, cs]).astype(b1_o.dtype)


def _rw_in(h, mix, vecs, wr, wk, wv, w1, w2, a1, a2, g1, g2, bd, *, n_rows, t_lat, t_ctx, n_lat):
    d = h.shape[1]
    tm = 256
    assert t_ctx == tm and t_lat % tm == 0 and tm % GRID_W == 0
    nb64 = n_rows // GRID_W
    c2 = lambda i: (0, 0)
    c3 = lambda i: (0, 0, 0)
    pm_spec = pl.BlockSpec((C_PAIRS, tm, LANES), lambda i: (0, i, 0))
    pm_dtypes = (BF16, BF16, BF16, BF16, F32, F32, BF16, BF16, BF16, BF16)
    pm_shapes = [jax.ShapeDtypeStruct((C_PAIRS, n_rows, LANES), dt) for dt in pm_dtypes]
    return pl.pallas_call(
        functools.partial(_rw_in_kernel, tm=tm, t_lat=t_lat, t_ctx=t_ctx, n_lat=n_lat),
        grid=(n_rows // tm,),
        in_specs=[pl.BlockSpec((tm, d), lambda i: (i, 0)),
                  pl.BlockSpec((GRID_W, d), lambda i: (jnp.maximum(i * (tm // GRID_W) - 1, 0), 0)),
                  pl.BlockSpec((GRID_W, d), lambda i: (jnp.minimum((i + 1) * (tm // GRID_W), nb64 - 1), 0)),
                  pl.BlockSpec((6, d), c2), pl.BlockSpec((8, d), c2),
                  pl.BlockSpec((d, d), c2), pl.BlockSpec((d, d), c2), pl.BlockSpec((d, d), c2),
                  pl.BlockSpec((d, LANES), c2), pl.BlockSpec((2, 64, d), c3),
                  pl.BlockSpec((d, LANES), c2), pl.BlockSpec((2, 64, d), c3),
                  pl.BlockSpec((d, LANES), c2), pl.BlockSpec((LANES, d), c2),
                  pl.BlockSpec((LANES, LANES), c2)],
        out_specs=[pm_spec] * 10 + [pl.BlockSpec((tm, d), lambda i: (i, 0))],
        out_shape=pm_shapes + [jax.ShapeDtypeStruct((n_rows, d), BF16)],
        compiler_params=_cparams(("parallel",)),
        name="rw_in",
    )(h, h, h, mix, vecs, wr, wk, wv, w1, w2, a1, a2, g1, g2, bd)


def _rwkv_kernel(*refs):
    ins_f, ins_b = refs[0:6], refs[6:12]
    y_f, y_b, st = refs[12:]

    @pl.when(pl.program_id(1) == 0)
    def _():
        st[...] = jnp.zeros_like(st)

    L = C_CHUNK
    dn = ((1,), (0,))
    t_i = _iota((L, LANES), 0)
    s_i = _iota((L, LANES), 1) % C_HEAD
    incl = (t_i >= s_i, t_i <= s_i)
    strict = (t_i > s_i, t_i < s_i)
    lo = _iota((L, LANES), 1) < C_HEAD
    eye = jnp.where(_iota((2 * L, 2 * L), 0) == _iota((2 * L, 2 * L), 1), 1.0, 0.0)
    same_head = (_iota((LANES, LANES), 0) < C_HEAD) == (_iota((LANES, LANES), 1) < C_HEAD)

    refs_d = (ins_f, ins_b)
    y_refs = (y_f, y_b)

    def group_program(chains):
        n = len(chains)
        cs = [_cumsum_rows(refs_d[d][1][p], d == 1) for d, p in chains]
        res0, res1, v0s, v1s, ar0, bke, dec = [], [], [], [], [], [], []
        for (d, p), c in zip(chains, cs):
            r_ref, lw_ref, k_ref, v_ref, kk_ref, b_ref = refs_d[d]
            r, k, v, b = [x[p].astype(F32) for x in (r_ref, k_ref, v_ref, b_ref)]
            lw = lw_ref[p]
            a = -kk_ref[p].astype(F32)
            cex = c - lw
            if d == 0:
                mid, c_end = c[L // 2 - 1:L // 2, :], c[L - 1:L, :]
            else:
                mid, c_end = c[L // 2:L // 2 + 1, :], c[0:1, :]
            e_k = jnp.exp(mid - c)
            e_end = jnp.exp(c_end - c)
            at = a * jnp.exp(cex - mid)
            rt = r * jnp.exp(c - mid)
            bt = (b * e_k).astype(BF16)
            kt = (k * e_k).astype(BF16)
            lhs = jnp.concatenate([jnp.where(lo, at, 0.0), jnp.where(lo, rt, 0.0),
                                   jnp.where(lo, 0.0, at), jnp.where(lo, 0.0, rt)], axis=0)
            res = _mm_nt(lhs, jnp.concatenate([bt, kt], axis=0))
            res0.append(res[:2 * L])
            res1.append(pltpu.roll(res[2 * L:], C_HEAD, 1))
            ar0.append(jnp.concatenate([a * jnp.exp(cex), r * jnp.exp(c)], axis=0).astype(BF16))
            bke.append(jnp.concatenate([b * e_end, k * e_end], axis=0).astype(BF16))
            v0s.append(jnp.where(lo, v, 0.0).astype(BF16))
            v1s.append(jnp.where(lo, 0.0, v).astype(BF16))
            dec.append(jnp.exp(c_end))
        s_prev = [st[d, p] for d, p in chains]
        sa = [_mm_nt(x, s) for x, s in zip(ar0, s_prev)]
        yield
        qs, k_sw, m_rb = [], [], []
        for (d, _), x0, x1 in zip(chains, res0, res1):
            top0 = jnp.where(strict[d], x0[:L], 0.0)
            top1 = jnp.where(strict[d], x1[:L], 0.0)
            bot0 = jnp.where(incl[d], x0[L:], 0.0)
            bot1 = jnp.where(incl[d], x1[L:], 0.0)
            qs.append(jnp.concatenate([jnp.where(lo, top0, 0.0), jnp.where(lo, 0.0, top1)], axis=0))
            k_sw.append(jnp.concatenate([jnp.where(lo, 0.0, top0), jnp.where(lo, top1, 0.0),
                                         jnp.where(lo, 0.0, bot0), jnp.where(lo, bot1, 0.0)], axis=0).astype(BF16))
            m_rb.append(jnp.where(lo, bot0, bot1).astype(BF16))
        ps = [eye + q for q in qs]
        qb = [q.astype(BF16) for q in qs]
        qs = [_dot(q, q, dn) for q in qb]
        kv = [_dot(k_sw[i], jnp.concatenate([v1s[i], v0s[i]], axis=0), dn) for i in range(n)]
        rhs = [jnp.concatenate([jnp.where(lo, sa[i][:L], 0.0), jnp.where(lo, 0.0, sa[i][:L])], axis=0)
               + kv[i][:2 * L] for i in range(n)]
        yield
        for _ in range(4):
            qb = [q.astype(BF16) for q in qs]
            res = [_dot(jnp.concatenate([q, pp.astype(BF16)], axis=0), q, dn) for q, pp in zip(qb, ps)]
            qs = [r_[:2 * L] for r_ in res]
            ps = [pp + r_[2 * L:] for pp, r_ in zip(ps, res)]
            yield
        ps = [pp + _mm(pp, q) for pp, q in zip(ps, qs)]
        yield
        us = [_mm(ps[i], rhs[i]) for i in range(n)]
        yield
        ys = [sa[i][L:] + _dot(m_rb[i], us[i].astype(BF16), dn) + kv[i][2 * L:3 * L] + kv[i][3 * L:]
              for i in range(n)]
        for i, (d, p) in enumerate(chains):
            uv = jnp.concatenate([(us[i][:L] + us[i][L:]).astype(BF16), v0s[i] + v1s[i]], axis=0)
            upd = _dot(uv, bke[i], ((0,), (0,)))
            st[d, p] = s_prev[i] * dec[i] + jnp.where(same_head, upd, 0.0)
            y_refs[d][p] = ys[i].astype(y_refs[d].dtype)

    programs = [group_program([(d, p) for p in range(g0, g0 + RW_GROUP) for d in range(2)])
                for g0 in range(0, C_PAIRS, RW_GROUP)]
    started = 0
    while programs:
        started = min(started + 1, len(programs))
        for prog in list(programs[:started]):
            if next(prog, "done") == "done":
                programs.remove(prog)
                started -= 1


def _rwkv_scan(r, v, kk, lw, k, b, *, n_rows, t_lat, t_ctx, bsz):
    ch = C_CHUNK
    nlat, nctx = t_lat // ch, t_ctx // ch
    fwd = functools.partial(_scan_block, nlat=nlat, nctx=nctx, bsz=bsz, reverse=False)
    bwd = functools.partial(_scan_block, nlat=nlat, nctx=nctx, bsz=bsz, reverse=True)
    sf = pl.BlockSpec((C_PAIRS, ch, LANES), lambda bb, s: (0, fwd(bb, s), 0))
    sb = pl.BlockSpec((C_PAIRS, ch, LANES), lambda bb, s: (0, bwd(bb, s), 0))
    return pl.pallas_call(
        _rwkv_kernel,
        grid=(bsz, nlat + nctx),
        in_specs=[sf] * 6 + [sb] * 6,
        out_specs=[sf, sb],
        out_shape=[jax.ShapeDtypeStruct((C_PAIRS, n_rows, LANES), BF16)] * 2,
        scratch_shapes=[pltpu.VMEM((2, C_PAIRS, LANES, LANES), F32)],
        compiler_params=_cparams(("parallel", "arbitrary")),
        name="rwkv_scan",
    )(r, lw[0], k[0], v, kk, b[0], r, lw[1], k[1], v, kk, b[1])


def _rw_out_kernel(x_ref, m_ref, yf, yb, bv, g_ref, lnw, lnb, bd_ref, w_ref, o_ref):
    bd = bd_ref[...]
    parts = []
    for p in range(C_PAIRS):
        cs = slice(p * LANES, (p + 1) * LANES)
        y = yf[p].astype(F32) + yb[p].astype(F32)
        mu = _mm_exact_r(y, bd) * (1.0 / C_HEAD)
        yc = y - mu
        var = _mm_exact_r(yc * yc, bd) * (1.0 / C_HEAD)
        yn = yc * lax.rsqrt(var + LNX_EPS) * lnw[:, cs] + lnb[:, cs] + bv[p].astype(F32)
        parts.append((yn * g_ref[:, cs].astype(F32)).astype(BF16))
    cat = jnp.concatenate(parts, axis=-1)
    o_ref[...] = x_ref[...] + m_ref[5:6, :] * _dot(cat, w_ref[...], ((1,), (0,)))


def _rw_out(x, mod, yf, yb, bv, g, lnw, lnb, bd, w, *, n_rows, t_lat, bsz):
    d = x.shape[1]
    tm = 256
    row = lambda i: (i, 0)
    const = lambda i: (0, 0)
    pm = pl.BlockSpec((C_PAIRS, tm, LANES), lambda i: (0, i, 0))
    return pl.pallas_call(
        _rw_out_kernel,
        grid=(n_rows // tm,),
        in_specs=[pl.BlockSpec((tm, d), row),
                  pl.BlockSpec((None, N_ADA, d), lambda i: (_mod_row(i, tm, t_lat, bsz), 0, 0)),
                  pm, pm, pm, pl.BlockSpec((tm, d), row),
                  pl.BlockSpec((1, d), const), pl.BlockSpec((1, d), const),
                  pl.BlockSpec((LANES, LANES), const), pl.BlockSpec((d, d), const)],
        out_specs=pl.BlockSpec((tm, d), row),
        out_shape=jax.ShapeDtypeStruct((n_rows, d), F32),
        compiler_params=_cparams(("parallel",)),
        name="rw_out",
    )(x, mod, yf, yb, bv, g, lnw.reshape(1, d), lnb.reshape(1, d), bd, w)


def _even_layer(x, mod, g, wi, wo, layer, in_w, out_w, lb, hgrn_g, conv_w, conv_b, dt_bias, a_log, d_skip, ssd_g, dims):
    n_all, t_lat, t_ctx, bsz = dims["n_all"], dims["t_lat"], dims["t_ctx"], dims["bsz"]
    n_lat = bsz * t_lat
    x, h = _ffn_half(x, mod, g[0], wi, wo, (layer, 0), slot=0, n_rows=n_all, t_lat=t_lat, bsz=bsz,
                     post="prenorm", g2=g[1], h_dtype=BF16)
    w_pad = jnp.pad(in_w, ((0, 0), (0, EVEN_IN_PAD - EVEN_IN))).astype(BF16)
    dtb = jnp.pad(dt_bias.reshape(1, 2 * B_HEADS), ((0, 0), (0, LANES - 2 * B_HEADS)))
    qi, f, gz, xbc, dt = _even_in(h, w_pad, lb.reshape(1, 2 * A_DIM), dtb, conv_w, conv_b,
                                  n_rows=n_all, t_lat=t_lat, t_ctx=t_ctx, n_lat=n_lat)
    oaf, oab = _hgrn_scan(qi, f, n_rows=n_all, t_lat=t_lat, t_ctx=t_ctx, bsz=bsz)
    a_row = jnp.pad(-jnp.exp(a_log.astype(F32)).reshape(1, 2 * B_HEADS), ((0, 0), (0, LANES - 2 * B_HEADS)))
    dsk_row = jnp.repeat(d_skip.astype(F32), B_HEADDIM).reshape(1, B_DIM)
    e_np = np.zeros((2, LANES, B_DIM), np.float32)
    for dd in range(2):
        for hh in range(B_HEADS):
            e_np[dd, dd * B_HEADS + hh, hh * B_HEADDIM:(hh + 1) * B_HEADDIM] = 1.0
    ybf, ybb = _ssd_scan(xbc, dt, a_row, dsk_row, jnp.asarray(e_np, BF16),
                         n_rows=n_all, t_lat=t_lat, t_ctx=t_ctx, bsz=bsz)
    x = _even_out(x, mod, oaf, oab, ybf, ybb, gz, hgrn_g.reshape(1, A_DK), ssd_g.reshape(1, B_DIM),
                  out_w.astype(BF16), n_rows=n_all, t_lat=t_lat, bsz=bsz)
    return x


def _odd_layer(x, mod, g, wi, wo, layer, p, dims):
    n_all, t_lat, t_ctx, bsz = dims["n_all"], dims["t_lat"], dims["t_ctx"], dims["bsz"]
    n_lat = bsz * t_lat
    d = D_MODEL
    x, h = _ffn_half(x, mod, g[0], wi, wo, (layer, 0), slot=0, n_rows=n_all, t_lat=t_lat, bsz=bsz,
                     post="prenorm", g2=g[1], h_dtype=F32)
    vecs = jnp.concatenate([p["w0"], p["a0"], p["k_k"][None], p["k_a"][None], p["r_k"].reshape(1, d),
                            jnp.zeros((1, d), F32)], axis=0)
    w1 = jnp.concatenate([p["w1"][0], p["w1"][1]], axis=1).astype(BF16)
    a1 = jnp.concatenate([p["a1"][0], p["a1"][1]], axis=1).astype(BF16)
    bd_np = np.kron(np.eye(2, dtype=np.float32), np.ones((C_HEAD, C_HEAD), np.float32))
    bd = jnp.asarray(bd_np, BF16)
    outs = _rw_in(h, p["x_mix"], vecs, p["wr"].astype(BF16), p["wk"].astype(BF16), p["wv"].astype(BF16),
                  w1, p["w2"].astype(BF16), a1, p["a2"].astype(BF16), p["g1"].astype(BF16),
                  p["g2"].astype(BF16), bd, n_rows=n_all, t_lat=t_lat, t_ctx=t_ctx, n_lat=n_lat)
    r, v, kk, bv, lw0, lw1, k0, k1, b0, b1, gg = outs
    yf, yb = _rwkv_scan(r, v, kk, (lw0, lw1), (k0, k1), (b0, b1), n_rows=n_all, t_lat=t_lat, t_ctx=t_ctx, bsz=bsz)
    x = _rw_out(x, mod, yf, yb, bv, gg, p["ln_w"], p["ln_b"], bd, p["wo"].astype(BF16),
                n_rows=n_all, t_lat=t_lat, bsz=bsz)
    return x


def kernel(x, c, ctx, c_ctx, ada_w, ada_b, norm_g, ffn_wi, ffn_wo, final_g, hgrn_lb_logits, ev_in_w, ev_out_w, hgrn_norm_g, ssd_conv_w, ssd_conv_b, ssd_dt_bias, ssd_a_log, ssd_d, ssd_norm_g, rw_x_mix, rw_wr, rw_wk, rw_wv, rw_wo, rw_w0, rw_w1, rw_w2, rw_a0, rw_a1, rw_a2, rw_g1, rw_g2, rw_k_k, rw_k_a, rw_r_k, rw_ln_w, rw_ln_b):
    bsz, t_lat, d = x.shape
    t_ctx = ctx.shape[1]
    depth = ada_w.shape[0]
    n_lat = bsz * t_lat
    n_all = n_lat + bsz * t_ctx
    dims = dict(n_all=n_all, t_lat=t_lat, t_ctx=t_ctx, bsz=bsz)
    assert bsz + 1 <= 8

    xs = (x.reshape(n_lat, d), ctx.reshape(bsz * t_ctx, d))
    cvec = jnp.concatenate([c, c_ctx[None], jnp.zeros((8 - bsz - 1, d), F32)], axis=0)
    mod_all = _modulation(cvec, ada_w, ada_b).reshape(depth, 8, N_ADA, d)
    lb_all = jnp.cumsum(jax.nn.softmax(hgrn_lb_logits.astype(F32), axis=0), axis=0)
    wi_bf = ffn_wi.astype(BF16)
    wo_bf = ffn_wo.astype(BF16)

    for layer in range(depth):
        last = layer == depth - 1
        mod = mod_all[layer]
        g = norm_g[layer]
        if layer % 2 == 0:
            e = layer // 2
            xs = _even_layer(xs, mod, g, wi_bf, wo_bf, layer, ev_in_w[e], ev_out_w[e], lb_all[layer],
                             hgrn_norm_g[e], ssd_conv_w[e], ssd_conv_b[e], ssd_dt_bias[e], ssd_a_log[e],
                             ssd_d[e], ssd_norm_g[e], dims)
        else:
            o = layer // 2
            p = dict(x_mix=rw_x_mix[o], wr=rw_wr[o], wk=rw_wk[o], wv=rw_wv[o], wo=rw_wo[o], w0=rw_w0[o],
                     w1=rw_w1[o], w2=rw_w2[o], a0=rw_a0[o], a1=rw_a1[o], a2=rw_a2[o], g1=rw_g1[o], g2=rw_g2[o],
                     k_k=rw_k_k[o], k_a=rw_k_a[o], r_k=rw_r_k[o], ln_w=rw_ln_w[o], ln_b=rw_ln_b[o])
            xs = _odd_layer(xs, mod, g, wi_bf, wo_bf, layer, p, dims)
        if last:
            out = _ffn_half(xs, mod, g[2], wi_bf, wo_bf, (layer, 1), slot=2, n_rows=n_lat, t_lat=t_lat,
                            bsz=bsz, post="final", g2=final_g)
        else:
            xs = _ffn_half(xs, mod, g[2], wi_bf, wo_bf, (layer, 1), slot=2, n_rows=n_all, t_lat=t_lat,
                           bsz=bsz)
    return out.reshape(bsz, t_lat, d)
```

```python
import functools

import numpy as np
import jax
import jax.numpy as jnp
from jax import lax
from jax.experimental import pallas as pl
from jax.experimental.pallas import tpu as pltpu

F32 = jnp.float32
BF16 = jnp.bfloat16

D_MODEL = 1024
N_ADA = 9
FF_DIM = 2816
EPS = 1e-6
GRID_W = 64
A_HEADS = 4
A_DK = 128
A_DIM = A_HEADS * A_DK
A_CHUNK = 32
B_HEADS = 8
B_HEADDIM = 64
B_DIM = B_HEADS * B_HEADDIM
B_GROUPS = 2
B_STATE = 128
B_CHUNK = 128
B_XBC = B_DIM + 2 * B_GROUPS * B_STATE
EVEN_IN = A_DIM * 5 + B_DIM + B_XBC + 2 * B_HEADS
EVEN_IN_PAD = 4224
C_HEAD = 64
C_HEADS = D_MODEL // C_HEAD
C_PAIRS = C_HEADS // 2
C_CHUNK = 64
LNX_EPS = 64e-5
SSD_BLOCK = 256
RW_BLOCK = 256
RW_GROUP = 4
LANES = 128

VMEM_LIMIT = 56 * 1024 * 1024


def _cparams(sem):
    return pltpu.CompilerParams(dimension_semantics=sem, vmem_limit_bytes=VMEM_LIMIT)


def _sigmoid(x):
    return 1.0 / (1.0 + jnp.exp(-x))


def _silu(x):
    return x * _sigmoid(x)


def _softplus(x):
    return jnp.maximum(x, 0.0) + jnp.log1p(jnp.exp(-jnp.abs(x)))


def _dot(a, b, dims):
    return lax.dot_general(a, b, (dims, ((), ())), preferred_element_type=F32)


def _mm(a, b):
    return _dot(a.astype(BF16), b.astype(BF16), ((1,), (0,)))


def _mm_nt(a, b):
    return _dot(a.astype(BF16), b.astype(BF16), ((1,), (1,)))


def _mm_tn(a, b):
    return _dot(a.astype(BF16), b.astype(BF16), ((0,), (0,)))


def _split2(x):
    hi = x.astype(BF16)
    lo = (x - hi.astype(F32)).astype(BF16)
    return hi, lo


def _split3(x):
    x1 = x.astype(BF16)
    r1 = x - x1.astype(F32)
    x2 = r1.astype(BF16)
    x3 = (r1 - x2.astype(F32)).astype(BF16)
    return x1, x2, x3


def _cumsum_rows(x, reverse):
    n = x.shape[0]
    row = _iota(x.shape, 0)
    s = 1
    while s < n:
        if reverse:
            x = x + jnp.where(row < n - s, pltpu.roll(x, n - s, 0), 0.0)
        else:
            x = x + jnp.where(row >= s, pltpu.roll(x, s, 0), 0.0)
        s *= 2
    return x


def _mm_exact_r(x, m_bf):
    x1, x2, x3 = _split3(x)
    dn = ((1,), (0,))
    return _dot(x1, m_bf, dn) + _dot(x2, m_bf, dn) + _dot(x3, m_bf, dn)


def _head_sum(x, m_bf):
    hi, lo = _split2(x)
    dn = ((1,), (0,))
    return _dot(hi, m_bf, dn) + _dot(lo, m_bf, dn)


def _rms(x):
    return x * lax.rsqrt(jnp.mean(x * x, axis=-1, keepdims=True) + EPS)


def _iota(shape, dim):
    return lax.broadcasted_iota(jnp.int32, shape, dim)


def _mod_row(i, tm, t_lat, bsz):
    return jnp.minimum((i * tm) // t_lat, bsz)


def _seq_pos(r0, t_lat, t_ctx, n_lat):
    in_lat = r0 < n_lat
    pos = jnp.where(in_lat, r0 % t_lat, (r0 - n_lat) % t_ctx)
    slen = jnp.where(in_lat, t_lat, t_ctx)
    return in_lat, pos, slen


def _scan_block(b, s, nlat, nctx, bsz, reverse):
    if reverse:
        cblk = bsz * nlat + b * nctx + (nctx - 1 - s)
        lblk = b * nlat + (nlat - 1 - (s - nctx))
    else:
        cblk = bsz * nlat + b * nctx + s
        lblk = b * nlat + (s - nctx)
    return jnp.where(s < nctx, cblk, lblk)


def _mod_kernel(c_ref, w_ref, b_ref, o_ref):
    sc = _silu(c_ref[...])
    o_ref[...] = _mm(sc, w_ref[...]) + b_ref[...]


def _modulation(cvec, ada_w, ada_b):
    depth, d, n = ada_w.shape
    tn = 1536
    return pl.pallas_call(
        _mod_kernel,
        grid=(depth, n // tn),
        in_specs=[pl.BlockSpec((8, d), lambda l, j: (0, 0)),
                  pl.BlockSpec((None, d, tn), lambda l, j: (l, 0, j)),
                  pl.BlockSpec((None, 1, tn), lambda l, j: (l, 0, j))],
        out_specs=pl.BlockSpec((None, 8, tn), lambda l, j: (l, 0, j)),
        out_shape=jax.ShapeDtypeStruct((depth, 8, n), F32),
        compiler_params=_cparams(("arbitrary", "arbitrary")),
        name="ada_mod",
    )(cvec, ada_w, ada_b.reshape(depth, 1, n))


N_PRE = 8


def _ffn_kernel(*refs, slot, tf, post, n_lat_tiles, pre):
    if n_lat_tiles is None:
        x_ref = refs[0]
        x = x_ref[...]
    else:
        x = jnp.where(pl.program_id(0) >= n_lat_tiles, refs[1][...], refs[0][...])
        refs = refs[1:]
    if pre is not None:
        finish = _even_finish if pre == "even" else _rw_finish
        m5 = refs[1 + N_PRE][5:6, :]
        x = x + m5 * finish(*refs[1:1 + N_PRE])
        refs = refs[:1] + refs[1 + N_PRE:]
    m_ref, g_ref, wi_ref, wo_ref = refs[1:5]
    if post == "prenorm":
        g2_ref, o_ref, h_ref, acc = refs[5:]
    elif post == "final":
        g2_ref, o_ref, acc = refs[5:]
    else:
        o_ref, acc = refs[5:]
    dn = ((1,), (0,))
    nch = FF_DIM // tf
    h = _rms(x) * g_ref[...] * (1.0 + m_ref[3 * slot + 1:3 * slot + 2, :]) + m_ref[3 * slot:3 * slot + 1, :]
    h = h.astype(BF16)

    def gate_up(c):
        return (_dot(h, wi_ref[:, c * tf:(c + 1) * tf], dn),
                _dot(h, wi_ref[:, FF_DIM + c * tf:FF_DIM + (c + 1) * tf], dn))

    nxt = gate_up(0)
    for c in range(nch):
        gt, up = nxt
        if c + 1 < nch:
            nxt = gate_up(c + 1)
        part = _dot((_silu(gt) * up).astype(BF16), wo_ref[c * tf:(c + 1) * tf, :], dn)
        if c == 0:
            acc[...] = part
        else:
            acc[...] += part

    xn = x + 0.5 * m_ref[3 * slot + 2:3 * slot + 3, :] * acc[...]
    if post == "final":
        o_ref[...] = _rms(xn) * g2_ref[...]
    else:
        o_ref[...] = xn
    if post == "prenorm":
        h2 = _rms(xn) * g2_ref[...] * (1.0 + m_ref[4:5, :]) + m_ref[3:4, :]
        h_ref[...] = h2.astype(h_ref.dtype)


def _ffn_half(x, mod, g, wi_all, wo_all, sel, *, slot, n_rows, t_lat, bsz, post=None, g2=None, h_dtype=F32,
              pre=None):
    split = isinstance(x, tuple)
    d = D_MODEL
    tm = 512
    while n_rows % tm or t_lat % tm:
        tm //= 2
    const = lambda i: (0, 0)
    resident = pl.Buffered(1)
    if split:
        n_lat_tiles = x[0].shape[0] // tm
        x_specs = [pl.BlockSpec((tm, d), lambda i: (jnp.minimum(i, n_lat_tiles - 1), 0)),
                   pl.BlockSpec((tm, d), lambda i: (jnp.maximum(i - n_lat_tiles, 0), 0))]
        x_args = list(x)
    else:
        n_lat_tiles = None
        x_specs = [pl.BlockSpec((tm, d), lambda i: (i, 0))]
        x_args = [x]
    pre_kind = None
    if pre is not None:
        pre_kind, pre_args = pre
        assert len(pre_args) == N_PRE and not split
        x_specs += _even_finish_specs(tm) if pre_kind == "even" else _rw_finish_specs(tm)
        x_args += list(pre_args)
    in_specs = x_specs + [
        pl.BlockSpec((None, N_ADA, d), lambda i: (_mod_row(i, tm, t_lat, bsz), 0, 0)),
        pl.BlockSpec((1, d), const),
        pl.BlockSpec((None, None, d, 2 * FF_DIM), lambda i: sel + (0, 0), pipeline_mode=resident),
        pl.BlockSpec((None, None, FF_DIM, d), lambda i: sel + (0, 0), pipeline_mode=resident)]
    args = x_args + [mod, g.reshape(1, d), wi_all, wo_all]
    out_spec = pl.BlockSpec((tm, d), lambda i: (i, 0))
    out_shape = jax.ShapeDtypeStruct((n_rows, d), F32)
    if post is not None:
        in_specs.append(pl.BlockSpec((1, d), const))
        args.append(g2.reshape(1, d))
    if post == "prenorm":
        out_specs = [out_spec, out_spec]
        out_shapes = [out_shape, jax.ShapeDtypeStruct((n_rows, d), h_dtype)]
    else:
        out_specs, out_shapes = out_spec, out_shape
    return pl.pallas_call(
        functools.partial(_ffn_kernel, slot=slot, tf=256, post=post, n_lat_tiles=n_lat_tiles, pre=pre_kind),
        grid=(n_rows // tm,),
        in_specs=in_specs,
        out_specs=out_specs,
        out_shape=out_shapes,
        scratch_shapes=[pltpu.VMEM((tm, d), F32)],
        compiler_params=_cparams(("parallel",)),
        name="ffn_half",
    )(*args)


HALO = 16


def _even_in_kernel(h_ref, hp_ref, hn_ref, w_ref, lb_ref, dtb_ref, cw_ref, cb_ref,
                    qi_ref, f_ref, gz_ref, xbc_ref, dt_ref, *, tm, t_lat, t_ctx, n_lat):
    h = h_ref[...]
    dn = ((1,), (0,))
    qi = _dot(h, w_ref[:, 0:1024], dn)
    qi_ref[:, 0:512] = _silu(qi[:, 0:512]).astype(qi_ref.dtype)
    qi_ref[:, 512:1024] = qi[:, 512:1024].astype(qi_ref.dtype)
    lb = lb_ref[...]
    f_ref[...] = lb + (1.0 - lb) * _sigmoid(_dot(h, w_ref[:, 1024:2048], dn))
    gz_ref[...] = _silu(_dot(h, w_ref[:, 2048:3072], dn)).astype(gz_ref.dtype)
    dt_ref[...] = _softplus(_dot(h, w_ref[:, 4096:4224], dn) + dtb_ref[...])
    n_ext = tm + 2 * HALO
    xe = _dot(jnp.concatenate([hp_ref[...], h, hn_ref[...]], axis=0), w_ref[:, 3072:4096], dn)
    x = xe[HALO:HALO + tm]
    xp = pltpu.roll(xe, 1, 0)[HALO:HALO + tm]
    xn = pltpu.roll(xe, n_ext - 1, 0)[HALO:HALO + tm]
    r = pl.program_id(0) * tm + _iota((tm, 1), 0)
    in_lat = r < n_lat
    pos = jnp.where(in_lat, r % t_lat, (r - n_lat) % t_ctx)
    slen = jnp.where(in_lat, t_lat, t_ctx)
    xp = jnp.where(pos == 0, 0.0, xp)
    xn = jnp.where(pos == slen - 1, 0.0, xn)
    y = cw_ref[0:1, :] * xp + cw_ref[1:2, :] * x + cw_ref[2:3, :] * xn + cb_ref[...]
    xbc_ref[...] = _silu(y).astype(xbc_ref.dtype)


def _even_in(h, w, lb, dtb, conv_w, conv_b, *, n_rows, t_lat, t_ctx, n_lat):
    d = h.shape[1]
    tm = 512
    while n_rows % tm:
        tm //= 2
    nbh = n_rows // HALO
    widths = (1024, 1024, 1024, B_XBC, LANES)
    dtypes = (BF16, F32, BF16, BF16, F32)
    const = lambda i: (0, 0)
    return pl.pallas_call(
        functools.partial(_even_in_kernel, tm=tm, t_lat=t_lat, t_ctx=t_ctx, n_lat=n_lat),
        grid=(n_rows // tm,),
        in_specs=[pl.BlockSpec((tm, d), lambda i: (i, 0)),
                  pl.BlockSpec((HALO, d), lambda i: (jnp.maximum(i * (tm // HALO) - 1, 0), 0)),
                  pl.BlockSpec((HALO, d), lambda i: (jnp.minimum((i + 1) * (tm // HALO), nbh - 1), 0)),
                  pl.BlockSpec((d, EVEN_IN_PAD), const, pipeline_mode=pl.Buffered(1)),
                  pl.BlockSpec((1, 1024), const),
                  pl.BlockSpec((1, LANES), const),
                  pl.BlockSpec((3, B_XBC), const),
                  pl.BlockSpec((1, B_XBC), const)],
        out_specs=[pl.BlockSpec((tm, wd), lambda i: (i, 0)) for wd in widths],
        out_shape=[jax.ShapeDtypeStruct((n_rows, wd), dt) for wd, dt in zip(widths, dtypes)],
        compiler_params=_cparams(("parallel",)),
        name="even_in",
    )(h, h, h, w, lb, dtb, conv_w, conv_b.reshape(1, B_XBC))


def _hgrn_kernel(qi_f, f_f, qi_b, f_b, o_f, o_b, st, *, ch):
    @pl.when(pl.program_id(1) == 0)
    def _():
        st[...] = jnp.zeros_like(st)

    L = A_CHUNK
    nsub = ch // L
    ri = _iota((L, L), 0)
    ci = _iota((L, L), 1)
    incl = (ri >= ci, ri <= ci)
    qis, fs, outs = (qi_f, qi_b), (f_f, f_b), (o_f, o_b)

    chains = [(d, h) for h in range(A_HEADS) for d in range(2)]

    def rows_of(j, d):
        jj = j if d == 0 else nsub - 1 - j
        return slice(jj * L, (jj + 1) * L)

    def local_stage(j):
        out = []
        for d, h in chains:
            cs = slice(h * A_DK, (h + 1) * A_DK)
            rows = rows_of(j, d)
            q = qis[d][rows, cs].astype(F32)
            v = qis[d][rows, A_DIM + h * A_DK:A_DIM + (h + 1) * A_DK].astype(BF16)
            f = fs[d][rows, cs]
            k = 1.0 - f
            bc = _cumsum_rows(jnp.log(f), d == 1)
            if d == 0:
                blast, bref = bc[L - 1:L, :], bc[L // 2 - 1:L // 2, :]
            else:
                blast, bref = bc[0:1, :], bc[L // 2:L // 2 + 1, :]
            out.append(dict(
                v=v, dec=jnp.exp(blast), qe=(q * jnp.exp(bc)).astype(BF16),
                ds=_mm_tn(v, k * jnp.exp(blast - bc)),
                sc=_mm_nt(q * jnp.exp(bc - bref), k * jnp.exp(bref - bc))))
        return out

    states = [st[d, h] for d, h in chains]
    nxt = local_stage(0)
    for j in range(nsub):
        cur = nxt
        if j + 1 < nsub:
            nxt = local_stage(j + 1)
        o2s = [_mm_nt(c["qe"], s) for c, s in zip(cur, states)]
        states = [c["dec"] * s + c["ds"] for c, s in zip(cur, states)]
        o1s = [_mm(jnp.where(incl[d], c["sc"], 0.0), c["v"]) for (d, _), c in zip(chains, cur)]
        for i, (d, h) in enumerate(chains):
            outs[d][rows_of(j, d), h * A_DK:(h + 1) * A_DK] = (o1s[i] + o2s[i]).astype(outs[d].dtype)
    for i, (d, h) in enumerate(chains):
        st[d, h] = states[i]


def _hgrn_scan(qi, f, *, n_rows, t_lat, t_ctx, bsz):
    ch = 256
    nlat, nctx = t_lat // ch, t_ctx // ch
    fwd = functools.partial(_scan_block, nlat=nlat, nctx=nctx, bsz=bsz, reverse=False)
    bwd = functools.partial(_scan_block, nlat=nlat, nctx=nctx, bsz=bsz, reverse=True)
    return pl.pallas_call(
        functools.partial(_hgrn_kernel, ch=ch),
        grid=(bsz, nlat + nctx),
        in_specs=[pl.BlockSpec((ch, 2 * A_DIM), lambda b, s: (fwd(b, s), 0)),
                  pl.BlockSpec((ch, A_DIM), lambda b, s: (fwd(b, s), 0)),
                  pl.BlockSpec((ch, 2 * A_DIM), lambda b, s: (bwd(b, s), 0)),
                  pl.BlockSpec((ch, A_DIM), lambda b, s: (bwd(b, s), 1))],
        out_specs=[pl.BlockSpec((ch, A_DIM), lambda b, s: (fwd(b, s), 0)),
                   pl.BlockSpec((ch, A_DIM), lambda b, s: (bwd(b, s), 0))],
        out_shape=[jax.ShapeDtypeStruct((n_rows, A_DIM), BF16)] * 2,
        scratch_shapes=[pltpu.VMEM((2, A_HEADS, A_DK, A_DK), F32)],
        compiler_params=_cparams(("parallel", "arbitrary")),
        name="hgrn_scan",
    )(qi, f, qi, f)


def _ssd_kernel(x_f, dt_f, x_b, dt_b, a_ref, dsk_ref, e_ref, o_f, o_b, st, *, ch):
    @pl.when(pl.program_id(1) == 0)
    def _():
        st[...] = jnp.zeros_like(st)

    L = B_CHUNK
    nsub = ch // L
    hg = B_HEADS // B_GROUPS
    gw = hg * B_HEADDIM
    ri = _iota((L, L), 0)
    ci = _iota((L, L), 1)
    incl = (ri >= ci, ri <= ci)
    xs, dts, outs = (x_f, x_b), (dt_f, dt_b), (o_f, o_b)
    dg = [(d, g) for d in range(2) for g in range(B_GROUPS)]

    def rows_of(j, d):
        jj = j if d == 0 else nsub - 1 - j
        return slice(jj * L, (jj + 1) * L)

    def local_stage(j):
        rows = [rows_of(j, d) for d in range(2)]
        dt = [dts[d][rows[d], :] for d in range(2)]
        acum = [_cumsum_rows(dt[d] * a_ref[...], d == 1) for d in range(2)]
        acum_e = [_mm_exact_r(acum[d], e_ref[d]) for d in range(2)]
        dt_e = [_mm_exact_r(dt[d], e_ref[d]) for d in range(2)]
        alast_e = [acum_e[0][L - 1:L, :], acum_e[1][0:1, :]]
        acum_t = [acum[d].T for d in range(2)]
        xa = [xs[d][rows[d], 0:B_DIM].astype(F32) for d in range(2)]
        xdt = [xa[d] * dt_e[d] for d in range(2)]
        xdec = [xdt[d] * jnp.exp(alast_e[d] - acum_e[d]) for d in range(2)]
        bgs = [xs[d][rows[d], B_DIM + g * B_STATE:B_DIM + (g + 1) * B_STATE].astype(BF16) for d, g in dg]
        cgs = [xs[d][rows[d], B_DIM + (B_GROUPS + g) * B_STATE:B_DIM + (B_GROUPS + g + 1) * B_STATE].astype(BF16)
               for d, g in dg]
        cbs = [_mm_nt(cgs[i], bgs[i]) for i in range(len(dg))]
        ds = [_mm_tn(bgs[i], xdec[d][:, g * gw:(g + 1) * gw]) for i, (d, g) in enumerate(dg)]
        ys = []
        for i, (d, g) in enumerate(dg):
            for hh in range(hg):
                h = g * hg + hh
                col = d * B_HEADS + h
                seg = acum[d][:, col:col + 1] - acum_t[d][col:col + 1, :]
                lm = jnp.exp(jnp.where(incl[d], seg, -1e30))
                hs = slice(h * B_HEADDIM, (h + 1) * B_HEADDIM)
                y = _mm(cbs[i] * lm, xdt[d][:, hs])
                if d == 0:
                    y = y + dsk_ref[:, hs] * xa[d][:, hs]
                ys.append(y)
        return dict(cgs=cgs, ds=ds, ys=ys,
                    e_in=[jnp.exp(acum_e[d][:, g * gw:(g + 1) * gw]) for d, g in dg],
                    e_out=[jnp.exp(alast_e[d][:, g * gw:(g + 1) * gw]) for d, g in dg])

    states = [st[d, g] for d, g in dg]
    nxt = local_stage(0)
    for j in range(nsub):
        cur = nxt
        if j + 1 < nsub:
            nxt = local_stage(j + 1)
        y_off = [_mm(cur["cgs"][i], states[i]) * cur["e_in"][i] for i in range(len(dg))]
        states = [states[i] * cur["e_out"][i] + cur["ds"][i] for i in range(len(dg))]
        for i, (d, g) in enumerate(dg):
            for hh in range(hg):
                h = g * hg + hh
                y = cur["ys"][i * hg + hh] + y_off[i][:, hh * B_HEADDIM:(hh + 1) * B_HEADDIM]
                outs[d][rows_of(j, d), h * B_HEADDIM:(h + 1) * B_HEADDIM] = y.astype(outs[d].dtype)
    for i, (d, g) in enumerate(dg):
        st[d, g] = states[i]


def _ssd_scan(xbc, dt, a_row, dsk_row, e_mat, *, n_rows, t_lat, t_ctx, bsz):
    ch = SSD_BLOCK
    nlat, nctx = t_lat // ch, t_ctx // ch
    fwd = functools.partial(_scan_block, nlat=nlat, nctx=nctx, bsz=bsz, reverse=False)
    bwd = functools.partial(_scan_block, nlat=nlat, nctx=nctx, bsz=bsz, reverse=True)
    const2 = lambda b, s: (0, 0)
    return pl.pallas_call(
        functools.partial(_ssd_kernel, ch=ch),
        grid=(bsz, nlat + nctx),
        in_specs=[pl.BlockSpec((ch, B_XBC), lambda b, s: (fwd(b, s), 0)),
                  pl.BlockSpec((ch, LANES), lambda b, s: (fwd(b, s), 0)),
                  pl.BlockSpec((ch, B_XBC), lambda b, s: (bwd(b, s), 0)),
                  pl.BlockSpec((ch, LANES), lambda b, s: (bwd(b, s), 0)),
                  pl.BlockSpec((1, LANES), const2),
                  pl.BlockSpec((1, B_DIM), const2),
                  pl.BlockSpec((2, LANES, B_DIM), lambda b, s: (0, 0, 0))],
        out_specs=[pl.BlockSpec((ch, B_DIM), lambda b, s: (fwd(b, s), 0)),
                   pl.BlockSpec((ch, B_DIM), lambda b, s: (bwd(b, s), 0))],
        out_shape=[jax.ShapeDtypeStruct((n_rows, B_DIM), BF16)] * 2,
        scratch_shapes=[pltpu.VMEM((2, B_GROUPS, B_STATE, (B_HEADS // B_GROUPS) * B_HEADDIM), F32)],
        compiler_params=_cparams(("parallel", "arbitrary")),
        name="ssd_scan",
    )(xbc, dt, xbc, dt, a_row, dsk_row, e_mat)


def _even_finish(oaf, oab, ybf, ybb, gz_ref, ga_ref, gb_ref, w_ref):
    oa = oaf[...].astype(F32) + oab[...].astype(F32)
    gz = gz_ref[...].astype(F32)
    parts = []
    for h in range(A_HEADS):
        cs = slice(h * A_DK, (h + 1) * A_DK)
        parts.append(_rms(oa[:, cs]) * ga_ref[...] * gz[:, cs])
    ob = (ybf[...].astype(F32) + ybb[...].astype(F32)) * gz[:, A_DIM:A_DIM + B_DIM]
    gw = B_DIM // B_GROUPS
    for g in range(B_GROUPS):
        cs = slice(g * gw, (g + 1) * gw)
        parts.append(_rms(ob[:, cs]) * gb_ref[:, cs])
    cat = jnp.concatenate(parts, axis=-1).astype(BF16)
    return _dot(cat, w_ref[...], ((1,), (0,)))


def _even_finish_specs(tm):
    row = lambda i: (i, 0)
    const = lambda i: (0, 0)
    return [pl.BlockSpec((tm, A_DIM), row), pl.BlockSpec((tm, A_DIM), row),
            pl.BlockSpec((tm, B_DIM), row), pl.BlockSpec((tm, B_DIM), row),
            pl.BlockSpec((tm, A_DIM + B_DIM), row),
            pl.BlockSpec((1, A_DK), const), pl.BlockSpec((1, B_DIM), const),
            pl.BlockSpec((A_DIM + B_DIM, D_MODEL), const, pipeline_mode=pl.Buffered(1))]


def _rw_in_kernel(h_ref, up_ref, dn_ref, mix_ref, vec_ref, wr, wk, wv, w1, w2, a1, a2, g1, g2, bd_ref,
                  r_o, v_o, kk_o, bv_o, lw0_o, lw1_o, k0_o, k1_o, b0_o, b1_o, g_o,
                  *, tm, t_lat, t_ctx, n_lat):
    d = D_MODEL
    r0 = pl.program_id(0) * tm
    in_lat, pos, slen = _seq_pos(r0, t_lat, t_ctx, n_lat)
    first = pos == 0
    last = pos + tm == slen
    row = _iota((tm, 1), 0)
    col = row % GRID_W
    ok_prev = jnp.where(in_lat, col, row) != 0
    ok_next = jnp.where(in_lat, col - (GRID_W - 1), row - (tm - 1)) != 0
    q4 = d // 4

    def prev_of(x):
        return jnp.where(ok_prev, pltpu.roll(x, 1, 0), 0.0)

    def next_of(x):
        return jnp.where(ok_next, pltpu.roll(x, tm - 1, 0), 0.0)

    h = h_ref[...]
    h2, h3 = h[:, 2 * q4:3 * q4], h[:, 3 * q4:]
    up = jnp.concatenate([jnp.where(first, 0.0, up_ref[:, 2 * q4:3 * q4]), h2[:tm - GRID_W]], axis=0)
    down = jnp.concatenate([h3[GRID_W:], jnp.where(last, 0.0, dn_ref[:, 3 * q4:])], axis=0)
    hs = jnp.concatenate([prev_of(h[:, :q4]), next_of(h[:, q4:2 * q4]),
                          jnp.where(in_lat, up, prev_of(h2)),
                          jnp.where(in_lat, down, next_of(h3))], axis=1)
    xx = hs - h
    xr, xw, xk, xv, xa, xg = [(h + xx * mix_ref[j:j + 1, :]).astype(BF16) for j in range(6)]
    dn = ((1,), (0,))
    r = _dot(xr, wr[...], dn)
    k = _dot(xk, wk[...], dn)
    v = _dot(xv, wv[...], dn)
    g = _mm(_sigmoid(_dot(xg, g1[...], dn)), g2[...])
    tw = jnp.tanh(_dot(xw, w1[...], dn))
    ta = _dot(xa, a1[...], dn)
    w0, a0 = vec_ref[0:2, :], vec_ref[2:4, :]
    k_k, k_a, r_k = vec_ref[4:5, :], vec_ref[5:6, :], vec_ref[6:7, :]
    bd = bd_ref[...]

    lws, ks, bs, kks = [], [], [], []
    kkr = k * k_k
    for p in range(C_PAIRS):
        cs = slice(p * LANES, (p + 1) * LANES)
        ss = _head_sum(kkr[:, cs] * kkr[:, cs], bd)
        kks.append(kkr[:, cs] * lax.rsqrt(jnp.maximum(ss, 1e-24)))
    for dd in range(2):
        ls = slice(dd * 64, (dd + 1) * 64)
        zw = w0[dd:dd + 1, :] + _mm(tw[:, ls], w2[dd])
        lws.append(-np.float32(np.exp(-0.5)) * _sigmoid(zw))
        asig = _sigmoid(a0[dd:dd + 1, :] + _mm(ta[:, ls], a2[dd]))
        ks.append(k * (1.0 + (asig - 1.0) * k_a))
        bs.append(asig)
    rk = r * (ks[0] + ks[1]) * r_k
    g_o[...] = g.astype(g_o.dtype)
    for p in range(C_PAIRS):
        cs = slice(p * LANES, (p + 1) * LANES)
        r_o[p] = r[:, cs].astype(r_o.dtype)
        v_o[p] = v[:, cs].astype(v_o.dtype)
        kk_o[p] = kks[p].astype(kk_o.dtype)
        bv_o[p] = (_head_sum(rk[:, cs], bd) * v[:, cs]).astype(bv_o.dtype)
        lw0_o[p] = lws[0][:, cs]
        lw1_o[p] = lws[1][:, cs]
        k0_o[p] = ks[0][:, cs].astype(k0_o.dtype)
        k1_o[p] = ks[1][:, cs].astype(k1_o.dtype)
        b0_o[p] = (kks[p] * bs[0][:, cs]).astype(b0_o.dtype)
        b1_o[p] = (kks[p] * bs[1][:, cs]).astype(b1_o.dtype)


def _rw_in(h, mix, vecs, wr, wk, wv, w1, w2, a1, a2, g1, g2, bd, *, n_rows, t_lat, t_ctx, n_lat):
    d = h.shape[1]
    tm = 256
    assert t_ctx == tm and t_lat % tm == 0 and tm % GRID_W == 0
    nb64 = n_rows // GRID_W
    c2 = lambda i: (0, 0)
    c3 = lambda i: (0, 0, 0)
    pm_spec = pl.BlockSpec((C_PAIRS, tm, LANES), lambda i: (0, i, 0))
    pm_dtypes = (BF16, BF16, BF16, BF16, F32, F32, BF16, BF16, BF16, BF16)
    pm_shapes = [jax.ShapeDtypeStruct((C_PAIRS, n_rows, LANES), dt) for dt in pm_dtypes]
    return pl.pallas_call(
        functools.partial(_rw_in_kernel, tm=tm, t_lat=t_lat, t_ctx=t_ctx, n_lat=n_lat),
        grid=(n_rows // tm,),
        in_specs=[pl.BlockSpec((tm, d), lambda i: (i, 0)),
                  pl.BlockSpec((GRID_W, d), lambda i: (jnp.maximum(i * (tm // GRID_W) - 1, 0), 0)),
                  pl.BlockSpec((GRID_W, d), lambda i: (jnp.minimum((i + 1) * (tm // GRID_W), nb64 - 1), 0)),
                  pl.BlockSpec((6, d), c2), pl.BlockSpec((8, d), c2),
                  pl.BlockSpec((d, d), c2), pl.BlockSpec((d, d), c2), pl.BlockSpec((d, d), c2),
                  pl.BlockSpec((d, LANES), c2), pl.BlockSpec((2, 64, d), c3),
                  pl.BlockSpec((d, LANES), c2), pl.BlockSpec((2, 64, d), c3),
                  pl.BlockSpec((d, LANES), c2), pl.BlockSpec((LANES, d), c2),
                  pl.BlockSpec((LANES, LANES), c2)],
        out_specs=[pm_spec] * 10 + [pl.BlockSpec((tm, d), lambda i: (i, 0))],
        out_shape=pm_shapes + [jax.ShapeDtypeStruct((n_rows, d), BF16)],
        compiler_params=_cparams(("parallel",)),
        name="rw_in",
    )(h, h, h, mix, vecs, wr, wk, wv, w1, w2, a1, a2, g1, g2, bd)


def _rwkv_kernel(*refs, ch):
    ins_f, ins_b = refs[0:6], refs[6:12]
    y_f, y_b, st = refs[12:]

    @pl.when(pl.program_id(1) == 0)
    def _():
        st[...] = jnp.zeros_like(st)

    L = C_CHUNK
    nsub = ch // L
    dn = ((1,), (0,))
    t_i = _iota((L, LANES), 0)
    s_i = _iota((L, LANES), 1) % C_HEAD
    incl = (t_i >= s_i, t_i <= s_i)
    strict = (t_i > s_i, t_i < s_i)
    lo = _iota((L, LANES), 1) < C_HEAD
    eye = jnp.where(_iota((2 * L, 2 * L), 0) == _iota((2 * L, 2 * L), 1), 1.0, 0.0)
    same_head = (_iota((LANES, LANES), 0) < C_HEAD) == (_iota((LANES, LANES), 1) < C_HEAD)

    refs_d = (ins_f, ins_b)
    y_refs = (y_f, y_b)

    def rows_of(j, d):
        jj = j if d == 0 else nsub - 1 - j
        return slice(jj * L, (jj + 1) * L)

    def local_program(j, chains, out):
        n = len(chains)
        cs = [_cumsum_rows(refs_d[d][1][p, rows_of(j, d), :], d == 1) for d, p in chains]
        res0, res1, v0s, v1s, ar0, bke, dec = [], [], [], [], [], [], []
        for (d, p), c in zip(chains, cs):
            r_ref, lw_ref, k_ref, v_ref, kk_ref, b_ref = refs_d[d]
            rows = rows_of(j, d)
            r, k, v, b = [x[p, rows, :].astype(F32) for x in (r_ref, k_ref, v_ref, b_ref)]
            lw = lw_ref[p, rows, :]
            a = -kk_ref[p, rows, :].astype(F32)
            cex = c - lw
            if d == 0:
                mid, c_end = c[L // 2 - 1:L // 2, :], c[L - 1:L, :]
            else:
                mid, c_end = c[L // 2:L // 2 + 1, :], c[0:1, :]
            e_k = jnp.exp(mid - c)
            e_end = jnp.exp(c_end - c)
            at = a * jnp.exp(cex - mid)
            rt = r * jnp.exp(c - mid)
            bt = (b * e_k).astype(BF16)
            kt = (k * e_k).astype(BF16)
            lhs = jnp.concatenate([jnp.where(lo, at, 0.0), jnp.where(lo, rt, 0.0),
                                   jnp.where(lo, 0.0, at), jnp.where(lo, 0.0, rt)], axis=0)
            res = _mm_nt(lhs, jnp.concatenate([bt, kt], axis=0))
            res0.append(res[:2 * L])
            res1.append(pltpu.roll(res[2 * L:], C_HEAD, 1))
            ar0.append(jnp.concatenate([a * jnp.exp(cex), r * jnp.exp(c)], axis=0).astype(BF16))
            bke.append(jnp.concatenate([b * e_end, k * e_end], axis=0).astype(BF16))
            v0s.append(jnp.where(lo, v, 0.0).astype(BF16))
            v1s.append(jnp.where(lo, 0.0, v).astype(BF16))
            dec.append(jnp.exp(c_end))
        out.update(ar0=ar0, bke=bke, v0s=v0s, v1s=v1s, dec=dec)
        yield
        qs, k_sw, m_rb = [], [], []
        for (d, _), x0, x1 in zip(chains, res0, res1):
            top0 = jnp.where(strict[d], x0[:L], 0.0)
            top1 = jnp.where(strict[d], x1[:L], 0.0)
            bot0 = jnp.where(incl[d], x0[L:], 0.0)
            bot1 = jnp.where(incl[d], x1[L:], 0.0)
            qs.append(jnp.concatenate([jnp.where(lo, top0, 0.0), jnp.where(lo, 0.0, top1)], axis=0))
            k_sw.append(jnp.concatenate([jnp.where(lo, 0.0, top0), jnp.where(lo, top1, 0.0),
                                         jnp.where(lo, 0.0, bot0), jnp.where(lo, bot1, 0.0)], axis=0).astype(BF16))
            m_rb.append(jnp.where(lo, bot0, bot1).astype(BF16))
        ps = [eye + q for q in qs]
        qb = [q.astype(BF16) for q in qs]
        qs = [_dot(q, q, dn) for q in qb]
        out.update(m_rb=m_rb,
                   kv=[_dot(k_sw[i], jnp.concatenate([v1s[i], v0s[i]], axis=0), dn) for i in range(n)])
        yield
        for _ in range(4):
            qb = [q.astype(BF16) for q in qs]
            res = [(_dot(q, q, dn), _dot(pp.astype(BF16), q, dn)) for q, pp in zip(qb, ps)]
            qs = [r_[0] for r_ in res]
            ps = [pp + r_[1] for pp, r_ in zip(ps, res)]
            yield
        out.update(t_inv=[(pp + _mm(pp, q)).astype(BF16) for pp, q in zip(ps, qs)])

    def state_program(j, chains, loc, states):
        n = len(chains)
        sa = [_mm_nt(x, s) for x, s in zip(loc["ar0"], states)]
        yield
        rhs = [jnp.concatenate([jnp.where(lo, sa[i][:L], 0.0), jnp.where(lo, 0.0, sa[i][:L])], axis=0)
               + loc["kv"][i][:2 * L] for i in range(n)]
        us = [_dot(loc["t_inv"][i], rhs[i].astype(BF16), dn) for i in range(n)]
        yield
        for i, (d, p) in enumerate(chains):
            kv = loc["kv"][i]
            uv = jnp.concatenate([(us[i][:L] + us[i][L:]).astype(BF16), loc["v0s"][i] + loc["v1s"][i]], axis=0)
            upd = _dot(uv, loc["bke"][i], ((0,), (0,)))
            y = sa[i][L:] + _dot(loc["m_rb"][i], us[i].astype(BF16), dn) + kv[2 * L:3 * L] + kv[3 * L:]
            states[i] = states[i] * loc["dec"][i] + jnp.where(same_head, upd, 0.0)
            y_refs[d][p, rows_of(j, d), :] = y.astype(y_refs[d].dtype)

    def run_interleaved(progs):
        while progs:
            for prog in list(progs):
                if next(prog, "done") == "done":
                    progs.remove(prog)

    for g0 in range(0, C_PAIRS, RW_GROUP):
        chains = [(d, p) for p in range(g0, g0 + RW_GROUP) for d in range(2)]
        states = [st[d, p] for d, p in chains]
        nxt = {}
        run_interleaved([local_program(0, chains, nxt)])
        for j in range(nsub):
            cur, nxt = nxt, {}
            progs = [state_program(j, chains, cur, states)]
            if j + 1 < nsub:
                progs.insert(0, local_program(j + 1, chains, nxt))
            run_interleaved(progs)
        for i, (d, p) in enumerate(chains):
            st[d, p] = states[i]


def _rwkv_scan(r, v, kk, lw, k, b, *, n_rows, t_lat, t_ctx, bsz):
    ch = RW_BLOCK
    nlat, nctx = t_lat // ch, t_ctx // ch
    fwd = functools.partial(_scan_block, nlat=nlat, nctx=nctx, bsz=bsz, reverse=False)
    bwd = functools.partial(_scan_block, nlat=nlat, nctx=nctx, bsz=bsz, reverse=True)
    sf = pl.BlockSpec((C_PAIRS, ch, LANES), lambda bb, s: (0, fwd(bb, s), 0))
    sb = pl.BlockSpec((C_PAIRS, ch, LANES), lambda bb, s: (0, bwd(bb, s), 0))
    return pl.pallas_call(
        functools.partial(_rwkv_kernel, ch=ch),
        grid=(bsz, nlat + nctx),
        in_specs=[sf] * 6 + [sb] * 6,
        out_specs=[sf, sb],
        out_shape=[jax.ShapeDtypeStruct((C_PAIRS, n_rows, LANES), BF16)] * 2,
        scratch_shapes=[pltpu.VMEM((2, C_PAIRS, LANES, LANES), F32)],
        compiler_params=_cparams(("parallel", "arbitrary")),
        name="rwkv_scan",
    )(r, lw[0], k[0], v, kk, b[0], r, lw[1], k[1], v, kk, b[1])


def _rw_finish(yf, yb, bv, g_ref, lnw, lnb, bd_ref, w_ref):
    bd = bd_ref[...]
    parts = []
    for p in range(C_PAIRS):
        cs = slice(p * LANES, (p + 1) * LANES)
        y = yf[p].astype(F32) + yb[p].astype(F32)
        mu = _head_sum(y, bd) * (1.0 / C_HEAD)
        yc = y - mu
        var = _head_sum(yc * yc, bd) * (1.0 / C_HEAD)
        yn = yc * lax.rsqrt(var + LNX_EPS) * lnw[:, cs] + lnb[:, cs] + bv[p].astype(F32)
        parts.append((yn * g_ref[:, cs].astype(F32)).astype(BF16))
    cat = jnp.concatenate(parts, axis=-1)
    return _dot(cat, w_ref[...], ((1,), (0,)))


def _rw_finish_specs(tm):
    d = D_MODEL
    row = lambda i: (i, 0)
    const = lambda i: (0, 0)
    pm = pl.BlockSpec((C_PAIRS, tm, LANES), lambda i: (0, i, 0))
    return [pm, pm, pm, pl.BlockSpec((tm, d), row),
            pl.BlockSpec((1, d), const), pl.BlockSpec((1, d), const),
            pl.BlockSpec((LANES, LANES), const),
            pl.BlockSpec((d, d), const, pipeline_mode=pl.Buffered(1))]


def _even_layer(x, mod, g, wi, wo, layer, in_w, out_w, lb, hgrn_g, conv_w, conv_b, dt_bias, a_log, d_skip, ssd_g, dims):
    n_all, t_lat, t_ctx, bsz = dims["n_all"], dims["t_lat"], dims["t_ctx"], dims["bsz"]
    n_lat = bsz * t_lat
    x, h = _ffn_half(x, mod, g[0], wi, wo, (layer, 0), slot=0, n_rows=n_all, t_lat=t_lat, bsz=bsz,
                     post="prenorm", g2=g[1], h_dtype=BF16)
    w_pad = jnp.pad(in_w, ((0, 0), (0, EVEN_IN_PAD - EVEN_IN))).astype(BF16)
    dtb = jnp.pad(dt_bias.reshape(1, 2 * B_HEADS), ((0, 0), (0, LANES - 2 * B_HEADS)))
    qi, f, gz, xbc, dt = _even_in(h, w_pad, lb.reshape(1, 2 * A_DIM), dtb, conv_w, conv_b,
                                  n_rows=n_all, t_lat=t_lat, t_ctx=t_ctx, n_lat=n_lat)
    oaf, oab = _hgrn_scan(qi, f, n_rows=n_all, t_lat=t_lat, t_ctx=t_ctx, bsz=bsz)
    a_row = jnp.pad(-jnp.exp(a_log.astype(F32)).reshape(1, 2 * B_HEADS), ((0, 0), (0, LANES - 2 * B_HEADS)))
    dsk_row = jnp.repeat(d_skip.astype(F32), B_HEADDIM).reshape(1, B_DIM)
    e_np = np.zeros((2, LANES, B_DIM), np.float32)
    for dd in range(2):
        for hh in range(B_HEADS):
            e_np[dd, dd * B_HEADS + hh, hh * B_HEADDIM:(hh + 1) * B_HEADDIM] = 1.0
    ybf, ybb = _ssd_scan(xbc, dt, a_row, dsk_row, jnp.asarray(e_np, BF16),
                         n_rows=n_all, t_lat=t_lat, t_ctx=t_ctx, bsz=bsz)
    finish = ("even", (oaf, oab, ybf, ybb, gz, hgrn_g.reshape(1, A_DK), ssd_g.reshape(1, B_DIM),
                       out_w.astype(BF16)))
    return x, finish


def _odd_layer(x, mod, g, wi, wo, layer, p, dims):
    n_all, t_lat, t_ctx, bsz = dims["n_all"], dims["t_lat"], dims["t_ctx"], dims["bsz"]
    n_lat = bsz * t_lat
    d = D_MODEL
    x, h = _ffn_half(x, mod, g[0], wi, wo, (layer, 0), slot=0, n_rows=n_all, t_lat=t_lat, bsz=bsz,
                     post="prenorm", g2=g[1], h_dtype=F32)
    vecs = jnp.concatenate([p["w0"], p["a0"], p["k_k"][None], p["k_a"][None], p["r_k"].reshape(1, d),
                            jnp.zeros((1, d), F32)], axis=0)
    w1 = jnp.concatenate([p["w1"][0], p["w1"][1]], axis=1).astype(BF16)
    a1 = jnp.concatenate([p["a1"][0], p["a1"][1]], axis=1).astype(BF16)
    bd_np = np.kron(np.eye(2, dtype=np.float32), np.ones((C_HEAD, C_HEAD), np.float32))
    bd = jnp.asarray(bd_np, BF16)
    outs = _rw_in(h, p["x_mix"], vecs, p["wr"].astype(BF16), p["wk"].astype(BF16), p["wv"].astype(BF16),
                  w1, p["w2"].astype(BF16), a1, p["a2"].astype(BF16), p["g1"].astype(BF16),
                  p["g2"].astype(BF16), bd, n_rows=n_all, t_lat=t_lat, t_ctx=t_ctx, n_lat=n_lat)
    r, v, kk, bv, lw0, lw1, k0, k1, b0, b1, gg = outs
    yf, yb = _rwkv_scan(r, v, kk, (lw0, lw1), (k0, k1), (b0, b1), n_rows=n_all, t_lat=t_lat, t_ctx=t_ctx, bsz=bsz)
    finish = ("rw", (yf, yb, bv, gg, p["ln_w"].reshape(1, d), p["ln_b"].reshape(1, d), bd, p["wo"].astype(BF16)))
    return x, finish


def kernel(x, c, ctx, c_ctx, ada_w, ada_b, norm_g, ffn_wi, ffn_wo, final_g, hgrn_lb_logits, ev_in_w, ev_out_w, hgrn_norm_g, ssd_conv_w, ssd_conv_b, ssd_dt_bias, ssd_a_log, ssd_d, ssd_norm_g, rw_x_mix, rw_wr, rw_wk, rw_wv, rw_wo, rw_w0, rw_w1, rw_w2, rw_a0, rw_a1, rw_a2, rw_g1, rw_g2, rw_k_k, rw_k_a, rw_r_k, rw_ln_w, rw_ln_b):
    bsz, t_lat, d = x.shape
    t_ctx = ctx.shape[1]
    depth = ada_w.shape[0]
    n_lat = bsz * t_lat
    n_all = n_lat + bsz * t_ctx
    dims = dict(n_all=n_all, t_lat=t_lat, t_ctx=t_ctx, bsz=bsz)
    assert bsz + 1 <= 8

    xs = (x.reshape(n_lat, d), ctx.reshape(bsz * t_ctx, d))
    cvec = jnp.concatenate([c, c_ctx[None], jnp.zeros((8 - bsz - 1, d), F32)], axis=0)
    mod_all = _modulation(cvec, ada_w, ada_b).reshape(depth, 8, N_ADA, d)
    lb_all = jnp.cumsum(jax.nn.softmax(hgrn_lb_logits.astype(F32), axis=0), axis=0)
    wi_bf = ffn_wi.astype(BF16)
    wo_bf = ffn_wo.astype(BF16)

    for layer in range(depth):
        last = layer == depth - 1
        mod = mod_all[layer]
        g = norm_g[layer]
        if layer % 2 == 0:
            e = layer // 2
            xs, finish = _even_layer(xs, mod, g, wi_bf, wo_bf, layer, ev_in_w[e], ev_out_w[e], lb_all[layer],
                             hgrn_norm_g[e], ssd_conv_w[e], ssd_conv_b[e], ssd_dt_bias[e], ssd_a_log[e],
                             ssd_d[e], ssd_norm_g[e], dims)
        else:
            o = layer // 2
            p = dict(x_mix=rw_x_mix[o], wr=rw_wr[o], wk=rw_wk[o], wv=rw_wv[o], wo=rw_wo[o], w0=rw_w0[o],
                     w1=rw_w1[o], w2=rw_w2[o], a0=rw_a0[o], a1=rw_a1[o], a2=rw_a2[o], g1=rw_g1[o], g2=rw_g2[o],
                     k_k=rw_k_k[o], k_a=rw_k_a[o], r_k=rw_r_k[o], ln_w=rw_ln_w[o], ln_b=rw_ln_b[o])
            xs, finish = _odd_layer(xs, mod, g, wi_bf, wo_bf, layer, p, dims)
        if last:
            out = _ffn_half(xs, mod, g[2], wi_bf, wo_bf, (layer, 1), slot=2, n_rows=n_lat, t_lat=t_lat,
                            bsz=bsz, post="final", g2=final_g, pre=finish)
        else:
            xs = _ffn_half(xs, mod, g[2], wi_bf, wo_bf, (layer, 1), slot=2, n_rows=n_all, t_lat=t_lat,
                           bsz=bsz, pre=finish)
    return out.reshape(bsz, t_lat, d)
```

```python
import functools

import numpy as np
import jax
import jax.numpy as jnp
from jax import lax
from jax.experimental import pallas as pl
from jax.experimental.pallas import tpu as pltpu

F32 = jnp.float32
BF16 = jnp.bfloat16

D_MODEL = 1024
N_ADA = 9
FF_DIM = 2816
EPS = 1e-6
GRID_W = 64
A_HEADS = 4
A_DK = 128
A_DIM = A_HEADS * A_DK
A_CHUNK = 32
B_HEADS = 8
B_HEADDIM = 64
B_DIM = B_HEADS * B_HEADDIM
B_GROUPS = 2
B_STATE = 128
B_CHUNK = 128
B_XBC = B_DIM + 2 * B_GROUPS * B_STATE
EVEN_IN = A_DIM * 5 + B_DIM + B_XBC + 2 * B_HEADS
EVEN_IN_PAD = 4224
C_HEAD = 64
C_HEADS = D_MODEL // C_HEAD
C_PAIRS = C_HEADS // 2
C_CHUNK = 64
LNX_EPS = 64e-5
SSD_BLOCK = 256
RW_BLOCK = 256
RW_GROUP = 4
LANES = 128

VMEM_LIMIT = 56 * 1024 * 1024


def _cparams(sem):
    return pltpu.CompilerParams(dimension_semantics=sem, vmem_limit_bytes=VMEM_LIMIT)


def _sigmoid(x):
    return 1.0 / (1.0 + jnp.exp(-x))


def _silu(x):
    return x * _sigmoid(x)


def _softplus(x):
    return jnp.maximum(x, 0.0) + jnp.log1p(jnp.exp(-jnp.abs(x)))


def _dot(a, b, dims):
    return lax.dot_general(a, b, (dims, ((), ())), preferred_element_type=F32)


def _mm(a, b):
    return _dot(a.astype(BF16), b.astype(BF16), ((1,), (0,)))


def _mm_nt(a, b):
    return _dot(a.astype(BF16), b.astype(BF16), ((1,), (1,)))


def _mm_tn(a, b):
    return _dot(a.astype(BF16), b.astype(BF16), ((0,), (0,)))


def _split2(x):
    hi = x.astype(BF16)
    lo = (x - hi.astype(F32)).astype(BF16)
    return hi, lo


def _split3(x):
    x1 = x.astype(BF16)
    r1 = x - x1.astype(F32)
    x2 = r1.astype(BF16)
    x3 = (r1 - x2.astype(F32)).astype(BF16)
    return x1, x2, x3


def _cumsum_rows(x, reverse):
    n = x.shape[0]
    row = _iota(x.shape, 0)
    s = 1
    while s < n:
        if reverse:
            x = x + jnp.where(row < n - s, pltpu.roll(x, n - s, 0), 0.0)
        else:
            x = x + jnp.where(row >= s, pltpu.roll(x, s, 0), 0.0)
        s *= 2
    return x


def _mm_exact_r(x, m_bf):
    x1, x2, x3 = _split3(x)
    dn = ((1,), (0,))
    return _dot(x1, m_bf, dn) + _dot(x2, m_bf, dn) + _dot(x3, m_bf, dn)


def _head_sum(x, m_bf):
    hi, lo = _split2(x)
    dn = ((1,), (0,))
    return _dot(hi, m_bf, dn) + _dot(lo, m_bf, dn)


def _rms(x):
    return x * lax.rsqrt(jnp.mean(x * x, axis=-1, keepdims=True) + EPS)


def _iota(shape, dim):
    return lax.broadcasted_iota(jnp.int32, shape, dim)


def _mod_row(i, tm, t_lat, bsz):
    return jnp.minimum((i * tm) // t_lat, bsz)


def _seq_pos(r0, t_lat, t_ctx, n_lat):
    in_lat = r0 < n_lat
    pos = jnp.where(in_lat, r0 % t_lat, (r0 - n_lat) % t_ctx)
    slen = jnp.where(in_lat, t_lat, t_ctx)
    return in_lat, pos, slen


def _scan_block(b, s, nlat, nctx, bsz, reverse):
    if reverse:
        cblk = bsz * nlat + b * nctx + (nctx - 1 - s)
        lblk = b * nlat + (nlat - 1 - (s - nctx))
    else:
        cblk = bsz * nlat + b * nctx + s
        lblk = b * nlat + (s - nctx)
    return jnp.where(s < nctx, cblk, lblk)


def _mod_kernel(c_ref, w_ref, b_ref, o_ref):
    sc = _silu(c_ref[...])
    o_ref[...] = _mm(sc, w_ref[...]) + b_ref[...]


def _modulation(cvec, ada_w, ada_b):
    depth, d, n = ada_w.shape
    tn = 1536
    return pl.pallas_call(
        _mod_kernel,
        grid=(depth, n // tn),
        in_specs=[pl.BlockSpec((8, d), lambda l, j: (0, 0)),
                  pl.BlockSpec((None, d, tn), lambda l, j: (l, 0, j)),
                  pl.BlockSpec((None, 1, tn), lambda l, j: (l, 0, j))],
        out_specs=pl.BlockSpec((None, 8, tn), lambda l, j: (l, 0, j)),
        out_shape=jax.ShapeDtypeStruct((depth, 8, n), F32),
        compiler_params=_cparams(("arbitrary", "arbitrary")),
        name="ada_mod",
    )(cvec, ada_w, ada_b.reshape(depth, 1, n))


N_PRE = 8


def _ffn_kernel(*refs, slot, tf, post, n_lat_tiles, pre):
    if n_lat_tiles is None:
        x_ref = refs[0]
        x = x_ref[...]
    else:
        x = jnp.where(pl.program_id(0) >= n_lat_tiles, refs[1][...], refs[0][...])
        refs = refs[1:]
    if pre is not None:
        finish = _even_finish if pre == "even" else _rw_finish
        m5 = refs[1 + N_PRE][5:6, :]
        x = x + m5 * finish(*refs[1:1 + N_PRE])
        refs = refs[:1] + refs[1 + N_PRE:]
    m_ref, g_ref, wi_ref, wo_ref = refs[1:5]
    if post == "prenorm":
        g2_ref, o_ref, h_ref, acc = refs[5:]
    elif post == "final":
        g2_ref, o_ref, acc = refs[5:]
    else:
        o_ref, acc = refs[5:]
    dn = ((1,), (0,))
    nch = FF_DIM // tf
    h = _rms(x) * g_ref[...] * (1.0 + m_ref[3 * slot + 1:3 * slot + 2, :]) + m_ref[3 * slot:3 * slot + 1, :]
    h = h.astype(BF16)

    def gate_up(c):
        return (_dot(h, wi_ref[:, c * tf:(c + 1) * tf], dn),
                _dot(h, wi_ref[:, FF_DIM + c * tf:FF_DIM + (c + 1) * tf], dn))

    nxt = gate_up(0)
    for c in range(nch):
        gt, up = nxt
        if c + 1 < nch:
            nxt = gate_up(c + 1)
        part = _dot((_silu(gt) * up).astype(BF16), wo_ref[c * tf:(c + 1) * tf, :], dn)
        if c == 0:
            acc[...] = part
        else:
            acc[...] += part

    xn = x + 0.5 * m_ref[3 * slot + 2:3 * slot + 3, :] * acc[...]
    if post == "final":
        o_ref[...] = _rms(xn) * g2_ref[...]
    else:
        o_ref[...] = xn
    if post == "prenorm":
        h2 = _rms(xn) * g2_ref[...] * (1.0 + m_ref[4:5, :]) + m_ref[3:4, :]
        h_ref[...] = h2.astype(h_ref.dtype)


def _ffn_half(x, mod, g, wi_all, wo_all, sel, *, slot, n_rows, t_lat, bsz, post=None, g2=None, h_dtype=F32,
              pre=None):
    split = isinstance(x, tuple)
    d = D_MODEL
    tm = 512
    while n_rows % tm or t_lat % tm:
        tm //= 2
    const = lambda i: (0, 0)
    resident = pl.Buffered(1)
    if split:
        n_lat_tiles = x[0].shape[0] // tm
        x_specs = [pl.BlockSpec((tm, d), lambda i: (jnp.minimum(i, n_lat_tiles - 1), 0)),
                   pl.BlockSpec((tm, d), lambda i: (jnp.maximum(i - n_lat_tiles, 0), 0))]
        x_args = list(x)
    else:
        n_lat_tiles = None
        x_specs = [pl.BlockSpec((tm, d), lambda i: (i, 0))]
        x_args = [x]
    pre_kind = None
    if pre is not None:
        pre_kind, pre_args = pre
        assert len(pre_args) == N_PRE and not split
        x_specs += _even_finish_specs(tm) if pre_kind == "even" else _rw_finish_specs(tm)
        x_args += list(pre_args)
    in_specs = x_specs + [
        pl.BlockSpec((None, N_ADA, d), lambda i: (_mod_row(i, tm, t_lat, bsz), 0, 0)),
        pl.BlockSpec((1, d), const),
        pl.BlockSpec((None, None, d, 2 * FF_DIM), lambda i: sel + (0, 0), pipeline_mode=resident),
        pl.BlockSpec((None, None, FF_DIM, d), lambda i: sel + (0, 0), pipeline_mode=resident)]
    args = x_args + [mod, g.reshape(1, d), wi_all, wo_all]
    out_spec = pl.BlockSpec((tm, d), lambda i: (i, 0))
    out_shape = jax.ShapeDtypeStruct((n_rows, d), F32)
    if post is not None:
        in_specs.append(pl.BlockSpec((1, d), const))
        args.append(g2.reshape(1, d))
    if post == "prenorm":
        out_specs = [out_spec, out_spec]
        out_shapes = [out_shape, jax.ShapeDtypeStruct((n_rows, d), h_dtype)]
    else:
        out_specs, out_shapes = out_spec, out_shape
    return pl.pallas_call(
        functools.partial(_ffn_kernel, slot=slot, tf=256, post=post, n_lat_tiles=n_lat_tiles, pre=pre_kind),
        grid=(n_rows // tm,),
        in_specs=in_specs,
        out_specs=out_specs,
        out_shape=out_shapes,
        scratch_shapes=[pltpu.VMEM((tm, d), F32)],
        compiler_params=_cparams(("parallel",)),
        name="ffn_half",
    )(*args)


HALO = 16


def _even_in_kernel(h_ref, hp_ref, hn_ref, w_ref, lb_ref, dtb_ref, cw_ref, cb_ref,
                    qi_ref, f_ref, gz_ref, xbc_ref, dt_ref, *, tm, t_lat, t_ctx, n_lat):
    h = h_ref[...]
    dn = ((1,), (0,))
    n_ext = tm + 2 * HALO
    xe = _dot(jnp.concatenate([hp_ref[...], h, hn_ref[...]], axis=0), w_ref[:, 3072:4096], dn)
    dt_ref[...] = _softplus(_dot(h, w_ref[:, 4096:4224], dn) + dtb_ref[...])
    lb = lb_ref[...]
    f_ref[...] = lb + (1.0 - lb) * _sigmoid(_dot(h, w_ref[:, 1024:2048], dn))
    gz_ref[...] = _silu(_dot(h, w_ref[:, 2048:3072], dn)).astype(gz_ref.dtype)
    qi = _dot(h, w_ref[:, 0:1024], dn)
    qi_ref[:, 0:512] = _silu(qi[:, 0:512]).astype(qi_ref.dtype)
    qi_ref[:, 512:1024] = qi[:, 512:1024].astype(qi_ref.dtype)
    x = xe[HALO:HALO + tm]
    xp = pltpu.roll(xe, 1, 0)[HALO:HALO + tm]
    xn = pltpu.roll(xe, n_ext - 1, 0)[HALO:HALO + tm]
    r = pl.program_id(0) * tm + _iota((tm, 1), 0)
    in_lat = r < n_lat
    pos = jnp.where(in_lat, r % t_lat, (r - n_lat) % t_ctx)
    slen = jnp.where(in_lat, t_lat, t_ctx)
    xp = jnp.where(pos == 0, 0.0, xp)
    xn = jnp.where(pos == slen - 1, 0.0, xn)
    y = cw_ref[0:1, :] * xp + cw_ref[1:2, :] * x + cw_ref[2:3, :] * xn + cb_ref[...]
    xbc_ref[...] = _silu(y).astype(xbc_ref.dtype)


def _even_in(h, w, lb, dtb, conv_w, conv_b, *, n_rows, t_lat, t_ctx, n_lat):
    d = h.shape[1]
    tm = 512
    while n_rows % tm:
        tm //= 2
    nbh = n_rows // HALO
    widths = (1024, 1024, 1024, B_XBC, LANES)
    dtypes = (BF16, F32, BF16, BF16, F32)
    const = lambda i: (0, 0)
    return pl.pallas_call(
        functools.partial(_even_in_kernel, tm=tm, t_lat=t_lat, t_ctx=t_ctx, n_lat=n_lat),
        grid=(n_rows // tm,),
        in_specs=[pl.BlockSpec((tm, d), lambda i: (i, 0)),
                  pl.BlockSpec((HALO, d), lambda i: (jnp.maximum(i * (tm // HALO) - 1, 0), 0)),
                  pl.BlockSpec((HALO, d), lambda i: (jnp.minimum((i + 1) * (tm // HALO), nbh - 1), 0)),
                  pl.BlockSpec((d, EVEN_IN_PAD), const, pipeline_mode=pl.Buffered(1)),
                  pl.BlockSpec((1, 1024), const),
                  pl.BlockSpec((1, LANES), const),
                  pl.BlockSpec((3, B_XBC), const),
                  pl.BlockSpec((1, B_XBC), const)],
        out_specs=[pl.BlockSpec((tm, wd), lambda i: (i, 0)) for wd in widths],
        out_shape=[jax.ShapeDtypeStruct((n_rows, wd), dt) for wd, dt in zip(widths, dtypes)],
        compiler_params=_cparams(("parallel",)),
        name="even_in",
    )(h, h, h, w, lb, dtb, conv_w, conv_b.reshape(1, B_XBC))


def _hgrn_kernel(qi_f, f_f, qi_b, f_b, o_f, o_b, st, *, ch):
    @pl.when(pl.program_id(1) == 0)
    def _():
        st[...] = jnp.zeros_like(st)

    L = A_CHUNK
    nsub = ch // L
    ri = _iota((L, L), 0)
    ci = _iota((L, L), 1)
    incl = (ri >= ci, ri <= ci)
    qis, fs, outs = (qi_f, qi_b), (f_f, f_b), (o_f, o_b)

    chains = [(d, h) for h in range(A_HEADS) for d in range(2)]

    def rows_of(j, d):
        jj = j if d == 0 else nsub - 1 - j
        return slice(jj * L, (jj + 1) * L)

    def local_stage(j):
        out = []
        for d, h in chains:
            cs = slice(h * A_DK, (h + 1) * A_DK)
            rows = rows_of(j, d)
            q = qis[d][rows, cs].astype(F32)
            v = qis[d][rows, A_DIM + h * A_DK:A_DIM + (h + 1) * A_DK].astype(BF16)
            f = fs[d][rows, cs]
            k = 1.0 - f
            bc = _cumsum_rows(jnp.log(f), d == 1)
            if d == 0:
                blast, bref = bc[L - 1:L, :], bc[L // 2 - 1:L // 2, :]
            else:
                blast, bref = bc[0:1, :], bc[L // 2:L // 2 + 1, :]
            out.append(dict(
                v=v, dec=jnp.exp(blast), qe=(q * jnp.exp(bc)).astype(BF16),
                ds=_mm_tn(v, k * jnp.exp(blast - bc)),
                sc=_mm_nt(q * jnp.exp(bc - bref), k * jnp.exp(bref - bc))))
        return out

    states = [st[d, h] for d, h in chains]
    nxt = local_stage(0)
    for j in range(nsub):
        cur = nxt
        if j + 1 < nsub:
            nxt = local_stage(j + 1)
        o1s = [_mm(jnp.where(incl[d], c["sc"], 0.0), c["v"]) for (d, _), c in zip(chains, cur)]
        o2s = [_mm_nt(c["qe"], s) for c, s in zip(cur, states)]
        states = [c["dec"] * s + c["ds"] for c, s in zip(cur, states)]
        for i, (d, h) in enumerate(chains):
            outs[d][rows_of(j, d), h * A_DK:(h + 1) * A_DK] = (o1s[i] + o2s[i]).astype(outs[d].dtype)
    for i, (d, h) in enumerate(chains):
        st[d, h] = states[i]


def _hgrn_scan(qi, f, *, n_rows, t_lat, t_ctx, bsz):
    ch = 256
    nlat, nctx = t_lat // ch, t_ctx // ch
    fwd = functools.partial(_scan_block, nlat=nlat, nctx=nctx, bsz=bsz, reverse=False)
    bwd = functools.partial(_scan_block, nlat=nlat, nctx=nctx, bsz=bsz, reverse=True)
    return pl.pallas_call(
        functools.partial(_hgrn_kernel, ch=ch),
        grid=(bsz, nlat + nctx),
        in_specs=[pl.BlockSpec((ch, 2 * A_DIM), lambda b, s: (fwd(b, s), 0)),
                  pl.BlockSpec((ch, A_DIM), lambda b, s: (fwd(b, s), 0)),
                  pl.BlockSpec((ch, 2 * A_DIM), lambda b, s: (bwd(b, s), 0)),
                  pl.BlockSpec((ch, A_DIM), lambda b, s: (bwd(b, s), 1))],
        out_specs=[pl.BlockSpec((ch, A_DIM), lambda b, s: (fwd(b, s), 0)),
                   pl.BlockSpec((ch, A_DIM), lambda b, s: (bwd(b, s), 0))],
        out_shape=[jax.ShapeDtypeStruct((n_rows, A_DIM), BF16)] * 2,
        scratch_shapes=[pltpu.VMEM((2, A_HEADS, A_DK, A_DK), F32)],
        compiler_params=_cparams(("parallel", "arbitrary")),
        name="hgrn_scan",
    )(qi, f, qi, f)


def _ssd_kernel(x_f, dt_f, x_b, dt_b, a_ref, dsk_ref, e_ref, o_f, o_b, st, *, ch):
    @pl.when(pl.program_id(1) == 0)
    def _():
        st[...] = jnp.zeros_like(st)

    L = B_CHUNK
    nsub = ch // L
    hg = B_HEADS // B_GROUPS
    gw = hg * B_HEADDIM
    ri = _iota((L, L), 0)
    ci = _iota((L, L), 1)
    incl = (ri >= ci, ri <= ci)
    xs, dts, outs = (x_f, x_b), (dt_f, dt_b), (o_f, o_b)
    dg = [(d, g) for d in range(2) for g in range(B_GROUPS)]

    def rows_of(j, d):
        jj = j if d == 0 else nsub - 1 - j
        return slice(jj * L, (jj + 1) * L)

    def local_stage(j):
        rows = [rows_of(j, d) for d in range(2)]
        dt = [dts[d][rows[d], :] for d in range(2)]
        acum = [_cumsum_rows(dt[d] * a_ref[...], d == 1) for d in range(2)]
        acum_e = [_mm_exact_r(acum[d], e_ref[d]) for d in range(2)]
        dt_e = [_mm_exact_r(dt[d], e_ref[d]) for d in range(2)]
        alast_e = [acum_e[0][L - 1:L, :], acum_e[1][0:1, :]]
        acum_t = [acum[d].T for d in range(2)]
        xa = [xs[d][rows[d], 0:B_DIM].astype(F32) for d in range(2)]
        xdt = [xa[d] * dt_e[d] for d in range(2)]
        xdec = [xdt[d] * jnp.exp(alast_e[d] - acum_e[d]) for d in range(2)]
        bgs = [xs[d][rows[d], B_DIM + g * B_STATE:B_DIM + (g + 1) * B_STATE].astype(BF16) for d, g in dg]
        cgs = [xs[d][rows[d], B_DIM + (B_GROUPS + g) * B_STATE:B_DIM + (B_GROUPS + g + 1) * B_STATE].astype(BF16)
               for d, g in dg]
        cbs = [_mm_nt(cgs[i], bgs[i]) for i in range(len(dg))]
        ds =[_mm_tn(bgs[i], xdec[d][:, g * gw:(g + 1) * gw]) for i, (d, g) in enumerate(dg)]
        ys = []
        for i, (d, g) in enumerate(dg):
            for hh in range(hg):
                h = g * hg + hh
                col = d * B_HEADS + h
                seg = acum[d][:, col:col + 1] - acum_t[d][col:col + 1, :]
                lm = jnp.exp(jnp.where(incl[d], seg, -1e30))
                hs = slice(h * B_HEADDIM, (h + 1) * B_HEADDIM)
                y = _mm(cbs[i] * lm, xdt[d][:, hs])
                if d == 0:
                    y = y + dsk_ref[:, hs] * xa[d][:, hs]
                ys.append(y)
        return dict(cgs=cgs, ds=ds, ys=ys,
                    e_in=[jnp.exp(acum_e[d][:, g * gw:(g + 1) * gw]) for d, g in dg],
                    e_out=[jnp.exp(alast_e[d][:, g * gw:(g + 1) * gw]) for d, g in dg])

    states = [st[d, g] for d, g in dg]
    nxt = local_stage(0)
    for j in range(nsub):
        cur = nxt
        if j + 1 < nsub:
            nxt = local_stage(j + 1)
        y_off = [_mm(cur["cgs"][i], states[i]) * cur["e_in"][i] for i in range(len(dg))]
        states = [states[i] * cur["e_out"][i] + cur["ds"][i] for i in range(len(dg))]
        for i, (d, g) in enumerate(dg):
            for hh in range(hg):
                h = g * hg + hh
                y = cur["ys"][i * hg + hh] + y_off[i][:, hh * B_HEADDIM:(hh + 1) * B_HEADDIM]
                outs[d][rows_of(j, d), h * B_HEADDIM:(h + 1) * B_HEADDIM] = y.astype(outs[d].dtype)
    for i, (d, g) in enumerate(dg):
        st[d, g] = states[i]


def _ssd_scan(xbc, dt, a_row, dsk_row, e_mat, *, n_rows, t_lat, t_ctx, bsz):
    ch = SSD_BLOCK
    nlat, nctx = t_lat // ch, t_ctx // ch
    fwd = functools.partial(_scan_block, nlat=nlat, nctx=nctx, bsz=bsz, reverse=False)
    bwd = functools.partial(_scan_block, nlat=nlat, nctx=nctx, bsz=bsz, reverse=True)
    const2 = lambda b, s: (0, 0)
    return pl.pallas_call(
        functools.partial(_ssd_kernel, ch=ch),
        grid=(bsz, nlat + nctx),
        in_specs=[pl.BlockSpec((ch, B_XBC), lambda b, s: (fwd(b, s), 0)),
                  pl.BlockSpec((ch, LANES), lambda b, s: (fwd(b, s), 0)),
                  pl.BlockSpec((ch, B_XBC), lambda b, s: (bwd(b, s), 0)),
                  pl.BlockSpec((ch, LANES), lambda b, s: (bwd(b, s), 0)),
                  pl.BlockSpec((1, LANES), const2),
                  pl.BlockSpec((1, B_DIM), const2),
                  pl.BlockSpec((2, LANES, B_DIM), lambda b, s: (0, 0, 0))],
        out_specs=[pl.BlockSpec((ch, B_DIM), lambda b, s: (fwd(b, s), 0)),
                   pl.BlockSpec((ch, B_DIM), lambda b, s: (bwd(b, s), 0))],
        out_shape=[jax.ShapeDtypeStruct((n_rows, B_DIM), BF16)] * 2,
        scratch_shapes=[pltpu.VMEM((2, B_GROUPS, B_STATE, (B_HEADS // B_GROUPS) * B_HEADDIM), F32)],
        compiler_params=_cparams(("parallel", "arbitrary")),
        name="ssd_scan",
    )(xbc, dt, xbc, dt, a_row, dsk_row, e_mat)


def _even_finish(oaf, oab, ybf, ybb, gz_ref, ga_ref, gb_ref, w_ref):
    oa = oaf[...].astype(F32) + oab[...].astype(F32)
    gz = gz_ref[...].astype(F32)
    parts = []
    for h in range(A_HEADS):
        cs = slice(h * A_DK, (h + 1) * A_DK)
        parts.append(_rms(oa[:, cs]) * ga_ref[...] * gz[:, cs])
    ob = (ybf[...].astype(F32) + ybb[...].astype(F32)) * gz[:, A_DIM:A_DIM + B_DIM]
    gw = B_DIM // B_GROUPS
    for g in range(B_GROUPS):
        cs = slice(g * gw, (g + 1) * gw)
        parts.append(_rms(ob[:, cs]) * gb_ref[:, cs])
    cat = jnp.concatenate(parts, axis=-1).astype(BF16)
    return _dot(cat, w_ref[...], ((1,), (0,)))


def _even_finish_specs(tm):
    row = lambda i: (i, 0)
    const = lambda i: (0, 0)
    return [pl.BlockSpec((tm, A_DIM), row), pl.BlockSpec((tm, A_DIM), row),
            pl.BlockSpec((tm, B_DIM), row), pl.BlockSpec((tm, B_DIM), row),
            pl.BlockSpec((tm, A_DIM + B_DIM), row),
            pl.BlockSpec((1, A_DK), const), pl.BlockSpec((1, B_DIM), const),
            pl.BlockSpec((A_DIM + B_DIM, D_MODEL), const, pipeline_mode=pl.Buffered(1))]


def _rw_in_kernel(h_ref, up_ref, dn_ref, mix_ref, vec_ref, wr, wk, wv, w1, w2, a1, a2, g1, g2, bd_ref,
                  r_o, v_o, kk_o, bv_o, lw0_o, lw1_o, k0_o, k1_o, b0_o, b1_o, g_o,
                  *, tm, t_lat, t_ctx, n_lat):
    d = D_MODEL
    r0 = pl.program_id(0) * tm
    in_lat, pos, slen = _seq_pos(r0, t_lat, t_ctx, n_lat)
    first = pos == 0
    last = pos + tm == slen
    row = _iota((tm, 1), 0)
    col = row % GRID_W
    ok_prev = jnp.where(in_lat, col, row) != 0
    ok_next = jnp.where(in_lat, col - (GRID_W - 1), row - (tm - 1)) != 0
    q4 = d // 4

    def prev_of(x):
        return jnp.where(ok_prev, pltpu.roll(x, 1, 0), 0.0)

    def next_of(x):
        return jnp.where(ok_next, pltpu.roll(x, tm - 1, 0), 0.0)

    h = h_ref[...]
    h2, h3 = h[:, 2 * q4:3 * q4], h[:, 3 * q4:]
    up = jnp.concatenate([jnp.where(first, 0.0, up_ref[:, 2 * q4:3 * q4]), h2[:tm - GRID_W]], axis=0)
    down = jnp.concatenate([h3[GRID_W:], jnp.where(last, 0.0, dn_ref[:, 3 * q4:])], axis=0)
    hs = jnp.concatenate([prev_of(h[:, :q4]), next_of(h[:, q4:2 * q4]),
                          jnp.where(in_lat, up, prev_of(h2)),
                          jnp.where(in_lat, down, next_of(h3))], axis=1)
    xx = hs - h
    xr, xw, xk, xv, xa, xg = [(h + xx * mix_ref[j:j + 1, :]).astype(BF16) for j in range(6)]
    dn = ((1,), (0,))
    tw = jnp.tanh(_dot(xw, w1[...], dn))
    ta = _dot(xa, a1[...], dn)
    tg = _sigmoid(_dot(xg, g1[...], dn))
    k = _dot(xk, wk[...], dn)
    lora_w = [_mm(tw[:, dd * 64:(dd + 1) * 64], w2[dd]) for dd in range(2)]
    lora_a = [_mm(ta[:, dd * 64:(dd + 1) * 64], a2[dd]) for dd in range(2)]
    r = _dot(xr, wr[...], dn)
    v = _dot(xv, wv[...], dn)
    g = _mm(tg, g2[...])
    w0, a0 = vec_ref[0:2, :], vec_ref[2:4, :]
    k_k, k_a, r_k = vec_ref[4:5, :], vec_ref[5:6, :], vec_ref[6:7, :]
    bd = bd_ref[...]

    lws, ks, bs, kks = [], [], [], []
    kkr = k * k_k
    for p in range(C_PAIRS):
        cs = slice(p * LANES, (p + 1) * LANES)
        ss = _head_sum(kkr[:, cs] * kkr[:, cs], bd)
        kks.append(kkr[:, cs] * lax.rsqrt(jnp.maximum(ss, 1e-24)))
    for dd in range(2):
        zw = w0[dd:dd + 1, :] + lora_w[dd]
        lws.append(-np.float32(np.exp(-0.5)) * _sigmoid(zw))
        asig = _sigmoid(a0[dd:dd + 1, :] + lora_a[dd])
        ks.append(k * (1.0 + (asig - 1.0) * k_a))
        bs.append(asig)
    rk = r * (ks[0] + ks[1]) * r_k
    g_o[...] = g.astype(g_o.dtype)
    for p in range(C_PAIRS):
        cs = slice(p * LANES, (p + 1) * LANES)
        r_o[p] = r[:, cs].astype(r_o.dtype)
        v_o[p] = v[:, cs].astype(v_o.dtype)
        kk_o[p] = kks[p].astype(kk_o.dtype)
        bv_o[p] = (_head_sum(rk[:, cs], bd) * v[:, cs]).astype(bv_o.dtype)
        lw0_o[p] = lws[0][:, cs]
        lw1_o[p] = lws[1][:, cs]
        k0_o[p] = ks[0][:, cs].astype(k0_o.dtype)
        k1_o[p] = ks[1][:, cs].astype(k1_o.dtype)
        b0_o[p] = (kks[p] * bs[0][:, cs]).astype(b0_o.dtype)
        b1_o[p] = (kks[p] * bs[1][:, cs]).astype(b1_o.dtype)


def _rw_in(h, mix, vecs, wr, wk, wv, w1, w2, a1, a2, g1, g2, bd, *, n_rows, t_lat, t_ctx, n_lat):
    d = h.shape[1]
    tm = 256
    assert t_ctx == tm and t_lat % tm == 0 and tm % GRID_W == 0
    nb64 = n_rows // GRID_W
    c2 = lambda i: (0, 0)
    c3 = lambda i: (0, 0, 0)
    pm_spec = pl.BlockSpec((C_PAIRS, tm, LANES), lambda i: (0, i, 0))
    pm_dtypes = (BF16, BF16, BF16, BF16, F32, F32, BF16, BF16, BF16, BF16)
    pm_shapes = [jax.ShapeDtypeStruct((C_PAIRS, n_rows, LANES), dt) for dt in pm_dtypes]
    return pl.pallas_call(
        functools.partial(_rw_in_kernel, tm=tm, t_lat=t_lat, t_ctx=t_ctx, n_lat=n_lat),
        grid=(n_rows // tm,),
        in_specs=[pl.BlockSpec((tm, d), lambda i: (i, 0)),
                  pl.BlockSpec((GRID_W, d), lambda i: (jnp.maximum(i * (tm // GRID_W) - 1, 0), 0)),
                  pl.BlockSpec((GRID_W, d), lambda i: (jnp.minimum((i + 1) * (tm // GRID_W), nb64 - 1), 0)),
                  pl.BlockSpec((6, d), c2), pl.BlockSpec((8, d), c2),
                  pl.BlockSpec((d, d), c2), pl.BlockSpec((d, d), c2), pl.BlockSpec((d, d), c2),
                  pl.BlockSpec((d, LANES), c2), pl.BlockSpec((2, 64, d), c3),
                  pl.BlockSpec((d, LANES), c2), pl.BlockSpec((2, 64, d), c3),
                  pl.BlockSpec((d, LANES), c2), pl.BlockSpec((LANES, d), c2),
                  pl.BlockSpec((LANES, LANES), c2)],
        out_specs=[pm_spec] * 10 + [pl.BlockSpec((tm, d), lambda i: (i, 0))],
        out_shape=pm_shapes + [jax.ShapeDtypeStruct((n_rows, d), BF16)],
        compiler_params=_cparams(("parallel",)),
        name="rw_in",
    )(h, h, h, mix, vecs, wr, wk, wv, w1, w2, a1, a2, g1, g2, bd)


def _rwkv_kernel(*refs, ch):
    ins_f, ins_b = refs[0:6], refs[6:12]
    y_f, y_b, st = refs[12:]

    @pl.when(pl.program_id(1) == 0)
    def _():
        st[...] = jnp.zeros_like(st)

    L = C_CHUNK
    nsub = ch // L
    dn = ((1,), (0,))
    t_i = _iota((L, LANES), 0)
    s_i = _iota((L, LANES), 1) % C_HEAD
    incl = (t_i >= s_i, t_i <= s_i)
    strict = (t_i > s_i, t_i < s_i)
    lo = _iota((L, LANES), 1) < C_HEAD
    eye = jnp.where(_iota((2 * L, 2 * L), 0) == _iota((2 * L, 2 * L), 1), 1.0, 0.0)
    same_head = (_iota((LANES, LANES), 0) < C_HEAD) == (_iota((LANES, LANES), 1) < C_HEAD)

    refs_d = (ins_f, ins_b)
    y_refs = (y_f, y_b)

    def rows_of(j, d):
        jj = j if d == 0 else nsub - 1 - j
        return slice(jj * L, (jj + 1) * L)

    def local_program(j, chains, out):
        n = len(chains)
        cs = [_cumsum_rows(refs_d[d][1][p, rows_of(j, d), :], d == 1) for d, p in chains]
        res0, res1, v0s, v1s, ar0, bke, dec = [], [], [], [], [], [], []
        for (d, p), c in zip(chains, cs):
            r_ref, lw_ref, k_ref, v_ref, kk_ref, b_ref = refs_d[d]
            rows = rows_of(j, d)
            r, k, v, b = [x[p, rows, :].astype(F32) for x in (r_ref, k_ref, v_ref, b_ref)]
            lw = lw_ref[p, rows, :]
            a = -kk_ref[p, rows, :].astype(F32)
            cex = c - lw
            if d == 0:
                mid, c_end = c[L // 2 - 1:L // 2, :], c[L - 1:L, :]
            else:
                mid, c_end = c[L // 2:L // 2 + 1, :], c[0:1, :]
            e_k = jnp.exp(mid - c)
            e_end = jnp.exp(c_end - c)
            at = a * jnp.exp(cex - mid)
            rt = r * jnp.exp(c - mid)
            bt = (b * e_k).astype(BF16)
            kt = (k * e_k).astype(BF16)
            lhs = jnp.concatenate([jnp.where(lo, at, 0.0), jnp.where(lo, rt, 0.0),
                                   jnp.where(lo, 0.0, at), jnp.where(lo, 0.0, rt)], axis=0)
            res = _mm_nt(lhs, jnp.concatenate([bt, kt], axis=0))
            res0.append(res[:2 * L])
            res1.append(pltpu.roll(res[2 * L:], C_HEAD, 1))
            ar0.append(jnp.concatenate([a * jnp.exp(cex), r * jnp.exp(c)], axis=0).astype(BF16))
            bke.append(jnp.concatenate([b * e_end, k * e_end], axis=0).astype(BF16))
            v0s.append(jnp.where(lo, v, 0.0).astype(BF16))
            v1s.append(jnp.where(lo, 0.0, v).astype(BF16))
            dec.append(jnp.exp(c_end))
        out.update(ar0=ar0, bke=bke, v0s=v0s, v1s=v1s, dec=dec)
        yield
        qs, k_sw, m_rb = [], [], []
        for (d, _), x0, x1 in zip(chains, res0, res1):
            top0 = jnp.where(strict[d], x0[:L], 0.0)
            top1 = jnp.where(strict[d], x1[:L], 0.0)
            bot0 = jnp.where(incl[d], x0[L:], 0.0)
            bot1 = jnp.where(incl[d], x1[L:], 0.0)
            qs.append(jnp.concatenate([jnp.where(lo, top0, 0.0), jnp.where(lo, 0.0, top1)], axis=0))
            k_sw.append(jnp.concatenate([jnp.where(lo, 0.0, top0), jnp.where(lo, top1, 0.0),
                                         jnp.where(lo, 0.0, bot0), jnp.where(lo, bot1, 0.0)], axis=0).astype(BF16))
            m_rb.append(jnp.where(lo, bot0, bot1).astype(BF16))
        ps = [eye + q for q in qs]
        qb = [q.astype(BF16) for q in qs]
        qs = [_dot(q, q, dn) for q in qb]
        out.update(m_rb=m_rb,
                   kv=[_dot(k_sw[i], jnp.concatenate([v1s[i], v0s[i]], axis=0), dn) for i in range(n)])
        yield
        for _ in range(4):
            qb = [q.astype(BF16) for q in qs]
            res = [(_dot(q, q, dn), _dot(pp.astype(BF16), q, dn)) for q, pp in zip(qb, ps)]
            qs = [r_[0] for r_ in res]
            ps = [pp + r_[1] for pp, r_ in zip(ps, res)]
            yield
        out.update(t_inv=[(pp + _mm(pp, q)).astype(BF16) for pp, q in zip(ps, qs)])

    def state_program(j, chains, loc, states):
        n = len(chains)
        sa = [_mm_nt(x, s) for x, s in zip(loc["ar0"], states)]
        yield
        rhs = [jnp.concatenate([jnp.where(lo, sa[i][:L], 0.0), jnp.where(lo, 0.0, sa[i][:L])], axis=0)
               + loc["kv"][i][:2 * L] for i in range(n)]
        us = [_dot(loc["t_inv"][i], rhs[i].astype(BF16), dn) for i in range(n)]
        yield
        for i, (d, p) in enumerate(chains):
            kv = loc["kv"][i]
            uv = jnp.concatenate([(us[i][:L] + us[i][L:]).astype(BF16), loc["v0s"][i] + loc["v1s"][i]], axis=0)
            upd = _dot(uv, loc["bke"][i], ((0,), (0,)))
            y = sa[i][L:] + _dot(loc["m_rb"][i], us[i].astype(BF16), dn) + kv[2 * L:3 * L] + kv[3 * L:]
            states[i] = states[i] * loc["dec"][i] + jnp.where(same_head, upd, 0.0)
            y_refs[d][p, rows_of(j, d), :] = y.astype(y_refs[d].dtype)

    def run_interleaved(progs):
        while progs:
            for prog in list(progs):
                if next(prog, "done") == "done":
                    progs.remove(prog)

    for g0 in range(0, C_PAIRS, RW_GROUP):
        chains = [(d, p) for p in range(g0, g0 + RW_GROUP) for d in range(2)]
        states = [st[d, p] for d, p in chains]
        nxt = {}
        run_interleaved([local_program(0, chains, nxt)])
        for j in range(nsub):
            cur, nxt = nxt, {}
            progs = [state_program(j, chains, cur, states)]
            if j + 1 < nsub:
                progs.insert(0, local_program(j + 1, chains, nxt))
            run_interleaved(progs)
        for i, (d, p) in enumerate(chains):
            st[d, p] = states[i]


def _rwkv_scan(r, v, kk, lw, k, b, *, n_rows, t_lat, t_ctx, bsz):
    ch = RW_BLOCK
    nlat, nctx = t_lat // ch, t_ctx // ch
    fwd = functools.partial(_scan_block, nlat=nlat, nctx=nctx, bsz=bsz, reverse=False)
    bwd = functools.partial(_scan_block, nlat=nlat, nctx=nctx, bsz=bsz, reverse=True)
    sf = pl.BlockSpec((C_PAIRS, ch, LANES), lambda bb, s: (0, fwd(bb, s), 0))
    sb = pl.BlockSpec((C_PAIRS, ch, LANES), lambda bb, s: (0, bwd(bb, s), 0))
    return pl.pallas_call(
        functools.partial(_rwkv_kernel, ch=ch),
        grid=(bsz, nlat + nctx),
        in_specs=[sf] * 6 + [sb] * 6,
        out_specs=[sf, sb],
        out_shape=[jax.ShapeDtypeStruct((C_PAIRS, n_rows, LANES), BF16)] * 2,
        scratch_shapes=[pltpu.VMEM((2, C_PAIRS, LANES, LANES), F32)],
        compiler_params=_cparams(("parallel", "arbitrary")),
        name="rwkv_scan",
    )(r, lw[0], k[0], v, kk, b[0], r, lw[1], k[1], v, kk, b[1])


def _rw_finish(yf, yb, bv, g_ref, lnw, lnb, bd_ref, w_ref):
    bd = bd_ref[...]
    ys = [yf[p].astype(F32) + yb[p].astype(F32) for p in range(C_PAIRS)]
    ycs = [y - _head_sum(y, bd) * (1.0 / C_HEAD) for y in ys]
    variances = [_head_sum(yc * yc, bd) * (1.0 / C_HEAD) for yc in ycs]
    parts = []
    for p in range(C_PAIRS):
        cs = slice(p * LANES, (p + 1) * LANES)
        yn = ycs[p] * lax.rsqrt(variances[p] + LNX_EPS) * lnw[:, cs] + lnb[:, cs] + bv[p].astype(F32)
        parts.append((yn * g_ref[:, cs].astype(F32)).astype(BF16))
    cat = jnp.concatenate(parts, axis=-1)
    return _dot(cat, w_ref[...], ((1,), (0,)))


def _rw_finish_specs(tm):
    d = D_MODEL
    row = lambda i: (i, 0)
    const = lambda i: (0, 0)
    pm = pl.BlockSpec((C_PAIRS, tm, LANES), lambda i: (0, i, 0))
    return [pm, pm, pm, pl.BlockSpec((tm, d), row),
            pl.BlockSpec((1, d), const), pl.BlockSpec((1, d), const),
            pl.BlockSpec((LANES, LANES), const),
            pl.BlockSpec((d, d), const, pipeline_mode=pl.Buffered(1))]


def _even_layer(x, mod, g, wi, wo, layer, in_w, out_w, lb, hgrn_g, conv_w, conv_b, dt_bias, a_log, d_skip, ssd_g, dims):
    n_all, t_lat, t_ctx, bsz = dims["n_all"], dims["t_lat"], dims["t_ctx"], dims["bsz"]
    n_lat = bsz * t_lat
    x, h = _ffn_half(x, mod, g[0], wi, wo, (layer, 0), slot=0, n_rows=n_all, t_lat=t_lat, bsz=bsz,
                     post="prenorm", g2=g[1], h_dtype=BF16)
    w_pad = jnp.pad(in_w, ((0, 0), (0, EVEN_IN_PAD - EVEN_IN))).astype(BF16)
    dtb = jnp.pad(dt_bias.reshape(1, 2 * B_HEADS), ((0, 0), (0, LANES - 2 * B_HEADS)))
    qi, f, gz, xbc, dt = _even_in(h, w_pad, lb.reshape(1, 2 * A_DIM), dtb, conv_w, conv_b,
                                  n_rows=n_all, t_lat=t_lat, t_ctx=t_ctx, n_lat=n_lat)
    oaf, oab = _hgrn_scan(qi, f, n_rows=n_all, t_lat=t_lat, t_ctx=t_ctx, bsz=bsz)
    a_row = jnp.pad(-jnp.exp(a_log.astype(F32)).reshape(1, 2 * B_HEADS), ((0, 0), (0, LANES - 2 * B_HEADS)))
    dsk_row = jnp.repeat(d_skip.astype(F32), B_HEADDIM).reshape(1, B_DIM)
    e_np = np.zeros((2, LANES, B_DIM), np.float32)
    for dd in range(2):
        for hh in range(B_HEADS):
            e_np[dd, dd * B_HEADS + hh, hh * B_HEADDIM:(hh + 1) * B_HEADDIM] = 1.0
    ybf, ybb = _ssd_scan(xbc, dt, a_row, dsk_row, jnp.asarray(e_np, BF16),
                         n_rows=n_all, t_lat=t_lat, t_ctx=t_ctx, bsz=bsz)
    finish = ("even", (oaf, oab, ybf, ybb, gz, hgrn_g.reshape(1, A_DK), ssd_g.reshape(1, B_DIM),
                       out_w.astype(BF16)))
    return x, finish


def _odd_layer(x, mod, g, wi, wo, layer, p, dims):
    n_all, t_lat, t_ctx, bsz = dims["n_all"], dims["t_lat"], dims["t_ctx"], dims["bsz"]
    n_lat = bsz * t_lat
    d = D_MODEL
    x, h = _ffn_half(x, mod, g[0], wi, wo, (layer, 0), slot=0, n_rows=n_all, t_lat=t_lat, bsz=bsz,
                     post="prenorm", g2=g[1], h_dtype=F32)
    vecs = jnp.concatenate([p["w0"], p["a0"], p["k_k"][None], p["k_a"][None], p["r_k"].reshape(1, d),
                            jnp.zeros((1, d), F32)], axis=0)
    w1 = jnp.concatenate([p["w1"][0], p["w1"][1]], axis=1).astype(BF16)
    a1 = jnp.concatenate([p["a1"][0], p["a1"][1]], axis=1).astype(BF16)
    bd_np = np.kron(np.eye(2, dtype=np.float32), np.ones((C_HEAD, C_HEAD), np.float32))
    bd = jnp.asarray(bd_np, BF16)
    outs = _rw_in(h, p["x_mix"], vecs, p["wr"].astype(BF16), p["wk"].astype(BF16), p["wv"].astype(BF16),
                  w1, p["w2"].astype(BF16), a1, p["a2"].astype(BF16), p["g1"].astype(BF16),
                  p["g2"].astype(BF16), bd, n_rows=n_all, t_lat=t_lat, t_ctx=t_ctx, n_lat=n_lat)
    r, v, kk, bv, lw0, lw1, k0, k1, b0, b1, gg = outs
    yf, yb = _rwkv_scan(r, v, kk, (lw0, lw1), (k0, k1), (b0, b1), n_rows=n_all, t_lat=t_lat, t_ctx=t_ctx, bsz=bsz)
    finish = ("rw", (yf, yb, bv, gg, p["ln_w"].reshape(1, d), p["ln_b"].reshape(1, d), bd, p["wo"].astype(BF16)))
    return x, finish


def kernel(x, c, ctx, c_ctx, ada_w, ada_b, norm_g, ffn_wi, ffn_wo, final_g, hgrn_lb_logits, ev_in_w, ev_out_w, hgrn_norm_g, ssd_conv_w, ssd_conv_b, ssd_dt_bias, ssd_a_log, ssd_d, ssd_norm_g, rw_x_mix, rw_wr, rw_wk, rw_wv, rw_wo, rw_w0, rw_w1, rw_w2, rw_a0, rw_a1, rw_a2, rw_g1, rw_g2, rw_k_k, rw_k_a, rw_r_k, rw_ln_w, rw_ln_b):
    bsz, t_lat, d = x.shape
    t_ctx = ctx.shape[1]
    depth = ada_w.shape[0]
    n_lat = bsz * t_lat
    n_all = n_lat + bsz * t_ctx
    dims = dict(n_all=n_all, t_lat=t_lat, t_ctx=t_ctx, bsz=bsz)
    assert bsz + 1 <= 8

    xs = (x.reshape(n_lat, d), ctx.reshape(bsz * t_ctx, d))
    cvec = jnp.concatenate([c, c_ctx[None], jnp.zeros((8 - bsz - 1, d), F32)], axis=0)
    mod_all = _modulation(cvec, ada_w, ada_b).reshape(depth, 8, N_ADA, d)
    lb_all = jnp.cumsum(jax.nn.softmax(hgrn_lb_logits.astype(F32), axis=0), axis=0)
    wi_bf = ffn_wi.astype(BF16)
    wo_bf = ffn_wo.astype(BF16)

    for layer in range(depth):
        last = layer == depth - 1
        mod = mod_all[layer]
        g = norm_g[layer]
        if layer % 2 == 0:
            e = layer // 2
            xs, finish = _even_layer(xs, mod, g, wi_bf, wo_bf, layer, ev_in_w[e], ev_out_w[e], lb_all[layer],
                             hgrn_norm_g[e], ssd_conv_w[e], ssd_conv_b[e], ssd_dt_bias[e], ssd_a_log[e],
                             ssd_d[e], ssd_norm_g[e], dims)
        else:
            o = layer // 2
            p = dict(x_mix=rw_x_mix[o], wr=rw_wr[o], wk=rw_wk[o], wv=rw_wv[o], wo=rw_wo[o], w0=rw_w0[o],
                     w1=rw_w1[o], w2=rw_w2[o], a0=rw_a0[o], a1=rw_a1[o], a2=rw_a2[o], g1=rw_g1[o], g2=rw_g2[o],
                     k_k=rw_k_k[o], k_a=rw_k_a[o], r_k=rw_r_k[o], ln_w=rw_ln_w[o], ln_b=rw_ln_b[o])
            xs, finish = _odd_layer(xs, mod, g, wi_bf, wo_bf, layer, p, dims)
        if last:
            out = _ffn_half(xs, mod, g[2], wi_bf, wo_bf, (layer, 1), slot=2, n_rows=n_lat, t_lat=t_lat,
                            bsz=bsz, post="final", g2=final_g, pre=finish)
        else:
            xs = _ffn_half(xs, mod, g[2], wi_bf, wo_bf, (layer, 1), slot=2, n_rows=n_all, t_lat=t_lat,
                           bsz=bsz, pre=finish)
    return out.reshape(bsz, t_lat, d)
```
